```python
import jax, jax.numpy as jnp
from jax import lax
import numpy as np

D_MODEL = 1024
BATCH = 4
SEQ = 4096
DEPTH = 4

N_EVEN = (DEPTH + 1) // 2
N_ODD = DEPTH // 2
EPS = 1e-6
CHUNK = 128
A_WIDTH = D_MODEL // 2
A_HEADS = 8
A_HEAD_DIM = A_WIDTH // A_HEADS
B_WIDTH = D_MODEL // 2
POOL_WINDOWS = (2, 4, 8, 16)
B_GROUPS = len(POOL_WINDOWS)
B_GROUP_DIM = B_WIDTH // B_GROUPS
IN_E = 2 * A_WIDTH + B_WIDTH
MIX_E = A_WIDTH + B_WIDTH
C_WIDTH = D_MODEL
CONV_K = 31
D_FF_DENSE = 2816
N_EXPERTS = 8
TOP_K = 2
D_FF_EXPERT = 3584

kernel_name = "hybrid_gmlp_pool_conformer_moe_trunk"


def rmsnorm(x, g):
    xf = x.astype(jnp.float32)
    y = xf * lax.rsqrt(jnp.mean(xf * xf, axis=-1, keepdims=True) + EPS)
    return (y * g.astype(jnp.float32)).astype(x.dtype)


def layernorm(x, g, b):
    xf = x.astype(jnp.float32)
    mu = jnp.mean(xf, axis=-1, keepdims=True)
    xc = xf - mu
    var = jnp.mean(xc * xc, axis=-1, keepdims=True)
    y = xc * lax.rsqrt(var + EPS) * g.astype(jnp.float32) + b.astype(jnp.float32)
    return y.astype(x.dtype)


def swiglu(h, w_gate, w_up, w_down):
    return (jax.nn.silu(h @ w_gate) * (h @ w_up)) @ w_down


def chunked_sgu(u, v, g_v, w_s, b_s):
    bsz, s, _ = v.shape
    v = rmsnorm(v, g_v)
    vc = v.reshape(bsz, s // CHUNK, CHUNK, A_HEADS, A_HEAD_DIM)
    w = jnp.tril(w_s)
    mixed = jnp.einsum('hts,bnshd->bnthd', w, vc) + b_s.T[None, None, :, :, None]
    return u * mixed.reshape(bsz, s, A_WIDTH)


def pool_mixer(z, w_pool, scale):
    bsz, s, _ = z.shape
    zg = z.astype(jnp.float32).reshape(bsz, s, B_GROUPS, B_GROUP_DIM)
    cs = jnp.cumsum(zg, axis=1)
    pos = jnp.arange(1, s + 1, dtype=jnp.float32)
    means = []
    for g, win in enumerate(POOL_WINDOWS):
        csg = cs[:, :, g]
        lag = jnp.pad(csg, ((0, 0), (win, 0), (0, 0)))[:, :s]
        cnt = jnp.minimum(pos, jnp.float32(win))
        means.append((csg - lag) / cnt[None, :, None])
    mixed = (jnp.stack(means, axis=2) - zg).astype(z.dtype)
    y = jnp.einsum('bsgc,gcd->bsgd', mixed, w_pool).reshape(bsz, s, B_WIDTH)
    return y * scale


def conformer_conv(h, w_pw1, b_pw1, w_dw, b_dw, ln_g, ln_b, w_pw2, b_pw2):
    a = h @ w_pw1 + b_pw1
    val, gate = jnp.split(a, 2, axis=-1)
    x = val * jax.nn.sigmoid(gate)
    x = lax.conv_general_dilated(
        x, w_dw[:, None, :].astype(x.dtype), window_strides=(1,),
        padding=[(CONV_K - 1, 0)], dimension_numbers=('NWC', 'WIO', 'NWC'),
        feature_group_count=C_WIDTH) + b_dw
    x = jax.nn.silu(layernorm(x, ln_g, ln_b))
    return x @ w_pw2 + b_pw2


def moe_swiglu(h, w_router, w_gate, w_up, w_down):
    bsz, s, d = h.shape
    hf = h.reshape(bsz * s, d)
    logits = (hf @ w_router).astype(jnp.float32)
    top_v, top_i = lax.top_k(logits, TOP_K)
    gates = jax.nn.softmax(top_v, axis=-1)
    comb = jnp.sum(jax.nn.one_hot(top_i, N_EXPERTS, dtype=jnp.float32) * gates[..., None], axis=1)
    comb = comb.astype(h.dtype)
    y = jnp.zeros_like(hf)
    for e in range(N_EXPERTS):
        y = y + comb[:, e:e + 1] * swiglu(hf, w_gate[e], w_up[e], w_down[e])
    return y.reshape(bsz, s, d)


def setup_inputs(seed: int = 0) -> dict:
    key = jax.random.key(seed)
    ks = iter(jax.random.split(key, 40))
    f32 = jnp.float32

    def nrm(shape, fan_in):
        return jax.random.normal(next(ks), shape, f32) * (fan_in ** -0.5)

    def gain(shape):
        return 1.0 + 0.05 * jax.random.normal(next(ks), shape, f32)

    def bias(shape):
        return 0.02 * jax.random.normal(next(ks), shape, f32)

    tri = jnp.tril(jnp.ones((CHUNK, CHUNK), f32))
    return {
        "x": jax.random.normal(next(ks), (BATCH, SEQ, D_MODEL), f32),
        "norm_mix_e": gain((N_EVEN, D_MODEL)),
        "w_in_e": nrm((N_EVEN, D_MODEL, IN_E), D_MODEL),
        "sgu_norm": gain((N_EVEN, A_WIDTH)),
        "w_spatial": nrm((N_EVEN, A_HEADS, CHUNK, CHUNK), CHUNK) * tri,
        "b_spatial": gain((N_EVEN, A_HEADS, CHUNK)),
        "w_pool": nrm((N_EVEN, B_GROUPS, B_GROUP_DIM, B_GROUP_DIM), B_GROUP_DIM),
        "pool_scale": gain((N_EVEN, B_WIDTH)),
        "w_out_e": nrm((N_EVEN, MIX_E, D_MODEL), MIX_E),
        "norm_ffn_e": gain((N_EVEN, D_MODEL)),
        "w_gate_d": nrm((N_EVEN, D_MODEL, D_FF_DENSE), D_MODEL),
        "w_up_d": nrm((N_EVEN, D_MODEL, D_FF_DENSE), D_MODEL),
        "w_down_d": nrm((N_EVEN, D_FF_DENSE, D_MODEL), D_FF_DENSE),
        "norm_mix_o": gain((N_ODD, D_MODEL)),
        "w_pw1": nrm((N_ODD, D_MODEL, 2 * C_WIDTH), D_MODEL),
        "b_pw1": bias((N_ODD, 2 * C_WIDTH)),
        "w_dw": nrm((N_ODD, CONV_K, C_WIDTH), CONV_K),
        "b_dw": bias((N_ODD, C_WIDTH)),
        "ln_g": gain((N_ODD, C_WIDTH)),
        "ln_b": bias((N_ODD, C_WIDTH)),
        "w_pw2": nrm((N_ODD, C_WIDTH, D_MODEL), C_WIDTH),
        "b_pw2": bias((N_ODD, D_MODEL)),
        "norm_ffn_o": gain((N_ODD, D_MODEL)),
        "w_router": nrm((N_ODD, D_MODEL, N_EXPERTS), D_MODEL),
        "w_gate_x": nrm((N_ODD, N_EXPERTS, D_MODEL, D_FF_EXPERT), D_MODEL),
        "w_up_x": nrm((N_ODD, N_EXPERTS, D_MODEL, D_FF_EXPERT), D_MODEL),
        "w_down_x": nrm((N_ODD, N_EXPERTS, D_FF_EXPERT, D_MODEL), D_FF_EXPERT),
        "norm_final": gain((D_MODEL,)),
    }


def reference(x, norm_mix_e, w_in_e, sgu_norm, w_spatial, b_spatial, w_pool, pool_scale,
              w_out_e, norm_ffn_e, w_gate_d, w_up_d, w_down_d,
              norm_mix_o, w_pw1, b_pw1, w_dw, b_dw, ln_g, ln_b, w_pw2, b_pw2,
              norm_ffn_o, w_router, w_gate_x, w_up_x, w_down_x, norm_final):
    h = x
    for layer in range(DEPTH):
        i = layer // 2
        if layer % 2 == 0:
            hn = rmsnorm(h, norm_mix_e[i])
            proj = hn @ w_in_e[i]
            uv = jax.nn.gelu(proj[..., :2 * A_WIDTH])
            u, v = uv[..., :A_WIDTH], uv[..., A_WIDTH:]
            z = proj[..., 2 * A_WIDTH:]
            a_out = chunked_sgu(u, v, sgu_norm[i], w_spatial[i], b_spatial[i])
            b_out = pool_mixer(z, w_pool[i], pool_scale[i])
            h = h + jnp.concatenate([a_out, b_out], axis=-1) @ w_out_e[i]
            h = h + swiglu(rmsnorm(h, norm_ffn_e[i]), w_gate_d[i], w_up_d[i], w_down_d[i])
        else:
            hn = rmsnorm(h, norm_mix_o[i])
            h = h + conformer_conv(hn, w_pw1[i], b_pw1[i], w_dw[i], b_dw[i],
                                   ln_g[i], ln_b[i], w_pw2[i], b_pw2[i])
            h = h + moe_swiglu(rmsnorm(h, norm_ffn_o[i]), w_router[i],
                               w_gate_x[i], w_up_x[i], w_down_x[i])
    return rmsnorm(h, norm_final)
```

```python
import functools

import jax
import jax.numpy as jnp
from jax import lax
from jax.experimental import pallas as pl
from jax.experimental.pallas import tpu as pltpu

F32 = jnp.float32
BF16 = jnp.bfloat16
I32 = jnp.int32

EPS = 1e-6
CHUNK = 128
A_HEADS = 8
POOL_WINDOWS = (2, 4, 8, 16)
POOL_HALO = 16
CONV_K = 31
CONV_HALO = 32
N_EXPERTS = 8
LANES = 128
SUBLANES = 8
VMEM_LIMIT = 56 * 1024 * 1024


def _cparams(sem):
    return pltpu.CompilerParams(dimension_semantics=sem, vmem_limit_bytes=VMEM_LIMIT)


def _rms(x, g):
    return x * lax.rsqrt(jnp.mean(x * x, axis=-1, keepdims=True) + EPS) * g


def _gelu_tanh(x):
    c = 0.7978845608028654
    return 0.5 * x * (1.0 + jnp.tanh(c * (x + 0.044715 * (x * x * x))))


def _silu(x):
    return x * jax.nn.sigmoid(x)


def _dot(a, b):
    return jnp.dot(a, b, preferred_element_type=F32)


def _even_mixer_kernel(h_ref, gn_ref, win_ref, gv_ref, ws_ref, bst_ref, wp_ref, ps_ref, wout_ref,
                       o_ref, zs_ref, cat_ref, *, tile):
    s = pl.program_id(1)
    d = h_ref.shape[-1]
    aw = d // 2
    nc = tile // CHUNK
    h = h_ref[0]
    hn = _rms(h, gn_ref[...]).astype(BF16)
    proj = _dot(hn, win_ref[...])
    u = _gelu_tanh(proj[:, :aw])
    v = _rms(_gelu_tanh(proj[:, aw:2 * aw]), gv_ref[...])
    z = proj[:, 2 * aw:]

    row = lax.broadcasted_iota(I32, (CHUNK, CHUNK), 0)
    col = lax.broadcasted_iota(I32, (CHUNK, CHUNK), 1)
    causal = row >= col
    lane = lax.broadcasted_iota(I32, (CHUNK, LANES), 1)
    lo = lane < (LANES // 2)
    for j in range(aw // LANES):
        w_a = jnp.where(causal, ws_ref[2 * j], 0).astype(BF16)
        w_b = jnp.where(causal, ws_ref[2 * j + 1], 0).astype(BF16)
        lhs = jnp.concatenate([w_a, w_b], axis=1)
        cols = []
        for c in range(nc):
            vb = v[c * CHUNK:(c + 1) * CHUNK, j * LANES:(j + 1) * LANES]
            cols.append(jnp.concatenate([jnp.where(lo, vb, 0.0), jnp.where(lo, 0.0, vb)], axis=0))
        rhs = jnp.concatenate(cols, axis=1).astype(BF16)
        mixed = _dot(lhs, rhs)
        bias = jnp.where(lo, bst_ref[:, 2 * j:2 * j + 1], bst_ref[:, 2 * j + 1:2 * j + 2])
        for c in range(nc):
            ub = u[c * CHUNK:(c + 1) * CHUNK, j * LANES:(j + 1) * LANES]
            a_out = ub * (mixed[:, c * LANES:(c + 1) * LANES] + bias)
            cat_ref[c * CHUNK:(c + 1) * CHUNK, j * LANES:(j + 1) * LANES] = a_out.astype(BF16)

    @pl.when(s == 0)
    def _():
        zs_ref[0:POOL_HALO, :] = jnp.zeros((POOL_HALO, aw), F32)

    @pl.when(s > 0)
    def _():
        zs_ref[0:POOL_HALO, :] = zs_ref[tile:tile + POOL_HALO, :]

    zs_ref[POOL_HALO:POOL_HALO + tile, :] = z
    pos = s * tile + 1 + lax.broadcasted_iota(I32, (tile, LANES), 0)
    for g, win in enumerate(POOL_WINDOWS):
        gs = slice(g * LANES, (g + 1) * LANES)
        acc = zs_ref[POOL_HALO:POOL_HALO + tile, gs]
        for i in range(1, win):
            acc = acc + zs_ref[POOL_HALO - i:POOL_HALO - i + tile, gs]
        cnt = jnp.minimum(pos, win).astype(F32)
        mixed = acc / cnt - z[:, gs]
        y = _dot(mixed.astype(BF16), wp_ref[g]) * ps_ref[:, gs]
        cat_ref[:, aw + g * LANES:aw + (g + 1) * LANES] = y.astype(BF16)

    o_ref[0] = h + _dot(cat_ref[...], wout_ref[...])


def _even_mixer(h, gn, w_in, gv, ws, bs, wp, ps, w_out, *, tile):
    b, s, d = h.shape
    aw = d // 2
    const2 = lambda *_: (0, 0)
    const3 = lambda *_: (0, 0, 0)
    return pl.pallas_call(
        functools.partial(_even_mixer_kernel, tile=tile),
        grid=(b, s // tile),
        in_specs=[
            pl.BlockSpec((1, tile, d), lambda i, j: (i, j, 0)),
            pl.BlockSpec((1, d), const2),
            pl.BlockSpec((d, 3 * aw), const2),
            pl.BlockSpec((1, aw), const2),
            pl.BlockSpec((A_HEADS, CHUNK, CHUNK), const3),
            pl.BlockSpec((CHUNK, A_HEADS), const2),
            pl.BlockSpec((len(POOL_WINDOWS), LANES, LANES), const3),
            pl.BlockSpec((1, aw), const2),
            pl.BlockSpec((d, d), const2),
        ],
        out_specs=pl.BlockSpec((1, tile, d), lambda i, j: (i, j, 0)),
        out_shape=jax.ShapeDtypeStruct(h.shape, F32),
        scratch_shapes=[pltpu.VMEM((tile + POOL_HALO, aw), F32), pltpu.VMEM((tile, d), BF16)],
        compiler_params=_cparams(("arbitrary", "arbitrary")),
        name="even_mixer",
    )(h, gn.reshape(1, d), w_in.astype(BF16), gv.reshape(1, aw), ws, bs.T, wp.astype(BF16),
      ps.reshape(1, aw), w_out.astype(BF16))


def _dense_ffn_kernel(h_ref, gn_ref, wg_ref, wu_ref, wd_ref, o_ref, xb_ref, acc_ref):
    j = pl.program_id(1)

    @pl.when(j == 0)
    def _():
        xb_ref[...] = _rms(h_ref[...], gn_ref[...]).astype(BF16)

    x = xb_ref[...]
    a = (_silu(_dot(x, wg_ref[...])) * _dot(x, wu_ref[...])).astype(BF16)
    part = _dot(a, wd_ref[...])

    @pl.when(j == 0)
    def _():
        acc_ref[...] = part

    @pl.when(j > 0)
    def _():
        acc_ref[...] += part

    @pl.when(j == pl.num_programs(1) - 1)
    def _():
        o_ref[...] = h_ref[...] + acc_ref[...]


def _dense_ffn(h, gn, wg, wu, wd, *, tm, tf):
    n, d = h.shape
    f = wg.shape[1]
    return pl.pallas_call(
        _dense_ffn_kernel,
        grid=(n // tm, f // tf),
        in_specs=[
            pl.BlockSpec((tm, d), lambda i, j: (i, 0)),
            pl.BlockSpec((1, d), lambda i, j: (0, 0)),
            pl.BlockSpec((d, tf), lambda i, j: (0, j)),
            pl.BlockSpec((d, tf), lambda i, j: (0, j)),
            pl.BlockSpec((tf, d), lambda i, j: (j, 0)),
        ],
        out_specs=pl.BlockSpec((tm, d), lambda i, j: (i, 0)),
        out_shape=jax.ShapeDtypeStruct((n, d), F32),
        scratch_shapes=[pltpu.VMEM((tm, d), BF16), pltpu.VMEM((tm, d), F32)],
        compiler_params=_cparams(("arbitrary", "arbitrary")),
        name="dense_ffn",
    )(h, gn.reshape(1, d), wg.astype(BF16), wu.astype(BF16), wd.astype(BF16))


def _split3(x):
    hi = x.astype(BF16)
    lo = (x - hi.astype(F32)).astype(BF16)
    return hi, lo


def _odd_mixer_kernel(h_ref, gn_ref, w1_ref, b1_ref, wdw_ref, bdw_ref, lg_ref, lb_ref, w2_ref, b2_ref,
                      gf_ref, wr_ref, o_ref, hn_ref, ri_ref, rg_ref, xs_ref, ys_ref, *, tile):
    s = pl.program_id(1)
    d = h_ref.shape[-1]
    ncb = d // LANES
    h = h_ref[0]
    hn = _rms(h, gn_ref[...]).astype(BF16)
    a = _dot(hn, w1_ref[...]) + b1_ref[...]
    x = a[:, :d] * jax.nn.sigmoid(a[:, d:])

    @pl.when(s == 0)
    def _():
        xs_ref[:, 0:CONV_HALO, :] = jnp.zeros((ncb, CONV_HALO, LANES), F32)

    @pl.when(s > 0)
    def _():
        xs_ref[:, 0:CONV_HALO, :] = xs_ref[:, tile:tile + CONV_HALO, :]

    for cb in range(ncb):
        xs_ref[cb, CONV_HALO:CONV_HALO + tile, :] = x[:, cb * LANES:(cb + 1) * LANES]

    rb = 128
    base = CONV_HALO - (CONV_K - 1)

    def conv_block(cb, carry):
        for r0 in range(0, tile, rb):
            acc = jnp.zeros((rb, LANES), F32)
            for k in range(CONV_K):
                acc = acc + wdw_ref[cb, k:k + 1, :] * xs_ref[cb, r0 + base + k:r0 + base + k + rb, :]
            ys_ref[cb, r0:r0 + rb, :] = acc
        return carry

    lax.fori_loop(0, ncb, conv_block, 0)
    y = jnp.concatenate([ys_ref[cb] for cb in range(ncb)], axis=1) + bdw_ref[...]

    mu = jnp.mean(y, axis=-1, keepdims=True)
    yc = y - mu
    var = jnp.mean(yc * yc, axis=-1, keepdims=True)
    yn = _silu(yc * lax.rsqrt(var + EPS) * lg_ref[...] + lb_ref[...])
    h2 = h + _dot(yn.astype(BF16), w2_ref[...]) + b2_ref[...]
    o_ref[0] = h2

    hf = _rms(h2, gf_ref[...])
    hn_ref[...] = hf
    x_hi, x_lo = _split3(hf)
    w_hi, w_lo = _split3(wr_ref[...])
    nt = (((1,), (1,)), ((), ()))
    logits = (lax.dot_general(w_hi, x_hi, nt, preferred_element_type=F32)
              + lax.dot_general(w_lo, x_hi, nt, preferred_element_type=F32)
              + lax.dot_general(w_hi, x_lo, nt, preferred_element_type=F32))
    eidx = lax.broadcasted_iota(I32, logits.shape, 0).astype(F32)
    m1 = jnp.max(logits, axis=0, keepdims=True)
    i1 = jnp.min(jnp.where(logits == m1, eidx, float(N_EXPERTS)), axis=0, keepdims=True)
    rest = jnp.where(eidx == i1, -jnp.inf, logits)
    m2 = jnp.max(rest, axis=0, keepdims=True)
    i2 = jnp.min(jnp.where(rest == m2, eidx, float(N_EXPERTS)), axis=0, keepdims=True)
    e2 = jnp.exp(m2 - m1)
    den = 1.0 + e2
    ri_ref[...] = jnp.concatenate([i1, i2], axis=0).astype(I32)
    rg_ref[...] = jnp.concatenate([1.0 / den, e2 / den], axis=0)


def _odd_mixer(h, gn, w1, b1, wdw, bdw, lg, lb, w2, b2, gf, wr, *, tile):
    b, s, d = h.shape
    n = b * s
    ncb = d // LANES
    nst = s // tile
    const2 = lambda *_: (0, 0)
    const3 = lambda *_: (0, 0, 0)
    wdw_b = jnp.transpose(wdw.reshape(CONV_K, ncb, LANES), (1, 0, 2))
    return pl.pallas_call(
        functools.partial(_odd_mixer_kernel, tile=tile),
        grid=(b, nst),
        in_specs=[
            pl.BlockSpec((1, tile, d), lambda i, j: (i, j, 0)),
            pl.BlockSpec((1, d), const2),
            pl.BlockSpec((d, 2 * d), const2),
            pl.BlockSpec((1, 2 * d), const2),
            pl.BlockSpec((ncb, CONV_K, LANES), const3),
            pl.BlockSpec((1, d), const2),
            pl.BlockSpec((1, d), const2),
            pl.BlockSpec((1, d), const2),
            pl.BlockSpec((d, d), const2),
            pl.BlockSpec((1, d), const2),
            pl.BlockSpec((1, d), const2),
            pl.BlockSpec((N_EXPERTS, d), const2),
        ],
        out_specs=[
            pl.BlockSpec((1, tile, d), lambda i, j: (i, j, 0)),
            pl.BlockSpec((tile, d), lambda i, j: (i * nst + j, 0)),
            pl.BlockSpec((2, tile), lambda i, j: (0, i * nst + j)),
            pl.BlockSpec((2, tile), lambda i, j: (0, i * nst + j)),
        ],
        out_shape=[
            jax.ShapeDtypeStruct(h.shape, F32),
            jax.ShapeDtypeStruct((n, d), F32),
            jax.ShapeDtypeStruct((2, n), I32),
            jax.ShapeDtypeStruct((2, n), F32),
        ],
        scratch_shapes=[pltpu.VMEM((ncb, tile + CONV_HALO, LANES), F32), pltpu.VMEM((ncb, tile, LANES), F32)],
        compiler_params=_cparams(("arbitrary", "arbitrary")),
        name="odd_mixer",
    )(h, gn.reshape(1, d), w1.astype(BF16), b1.reshape(1, 2 * d), wdw_b, bdw.reshape(1, d),
      lg.reshape(1, d), lb.reshape(1, d), w2.astype(BF16), b2.reshape(1, d), gf.reshape(1, d), wr.T)


def _route_meta_kernel(ri_ref, pos_ref, tab_ref, *, tm):
    ri = ri_ref[...]
    i0, i1 = ri[0], ri[1]
    nb = i0.shape[0]
    shift = tm.bit_length() - 1
    r = lax.broadcasted_iota(I32, (LANES, LANES), 0)
    c = lax.broadcasted_iota(I32, (LANES, LANES), 1)
    upper = (r < c).astype(BF16)
    rb = lax.broadcasted_iota(I32, (nb, nb), 0)
    cb = lax.broadcasted_iota(I32, (nb, nb), 1)
    lower = (cb < rb).astype(BF16)
    lane = lax.broadcasted_iota(I32, (1, LANES), 1)
    pos0 = jnp.zeros(i0.shape, I32)
    pos1 = jnp.zeros(i0.shape, I32)
    goff = jnp.zeros((1, 1), I32)
    tile_expert = jnp.zeros((1, LANES), I32)
    starts = jnp.zeros((1, LANES), I32)
    counts = jnp.zeros((1, LANES), I32)
    for e in range(N_EXPERTS):
        m = jnp.logical_or(i0 == e, i1 == e).astype(F32)
        within = _dot(m.astype(BF16), upper)
        tot = jnp.broadcast_to(jnp.sum(m, axis=1, keepdims=True), (nb, LANES))
        blk = _dot(lower, tot.astype(BF16))
        cnt = jnp.sum(tot[:, 0:1], axis=0, keepdims=True).astype(I32)
        dest = goff + (blk + within).astype(I32)
        pos0 = jnp.where(i0 == e, dest, pos0)
        pos1 = jnp.where(i1 == e, dest, pos1)
        starts = jnp.where(lane == e, goff, starts)
        counts = jnp.where(lane == e, cnt, counts)
        goff = goff + (((cnt + (tm - 1)) >> shift) << shift)
        tile_expert = tile_expert + ((lane << shift) >= goff).astype(I32)
    pos_ref[0] = pos0
    pos_ref[1] = pos1
    tab_ref[0:1, :] = jnp.minimum(tile_expert, N_EXPERTS - 1)
    tab_ref[1:2, :] = jnp.broadcast_to(goff >> shift, (1, LANES))
    tab_ref[2:3, :] = starts
    tab_ref[3:4, :] = counts
    tab_ref[4:8, :] = jnp.zeros((4, LANES), I32)


def _route_meta(ri, *, tm):
    n = ri.shape[1]
    nb = n // LANES
    pos, tab = pl.pallas_call(
        functools.partial(_route_meta_kernel, tm=tm),
        out_shape=[jax.ShapeDtypeStruct((2, nb, LANES), I32), jax.ShapeDtypeStruct((8, LANES), I32)],
        compiler_params=pltpu.CompilerParams(vmem_limit_bytes=VMEM_LIMIT),
        name="route_meta",
    )(ri.reshape(2, nb, LANES))
    return pos.reshape(2 * n), tab


def _dispatch_kernel(pos_ref, st_ref, cn_ref, na_ref, x_ref, o_hbm, zero_ref, sem, zsem, *, n, tm, td):
    i = pl.program_id(0)
    zrows = zero_ref.shape[0]
    nbits = tm.bit_length() - 1

    @pl.when(i == 0)
    def _():
        zero_ref[...] = jnp.zeros(zero_ref.shape, F32)
        def fill(start, w):
            cp = pltpu.make_async_copy(zero_ref.at[pl.ds(0, w)], o_hbm.at[pl.ds(start, w)], zsem)
            cp.start()
            cp.wait()

        for e in range(N_EXPERTS):
            cnt = cn_ref[e]
            off = st_ref[e] + cnt
            lead = (-cnt) & (SUBLANES - 1)
            for q in range(SUBLANES - 1):
                @pl.when(q < lead)
                def _(off=off, q=q):
                    fill(off + q, 1)

            rest = ((-cnt) & (tm - 1)) - lead
            off8 = off + lead
            for b in range(SUBLANES.bit_length() - 1, nbits):
                size = 1 << b

                @pl.when((rest & size) != 0)
                def _(off8=off8, rest=rest, size=size):
                    start = off8 + (rest & (size - 1))
                    for q in range(0, size, zrows):
                        w = min(zrows, size - q)
                        fill(pl.multiple_of(start + q, SUBLANES), w)

        for t in range(2 * n // tm, o_hbm.shape[0] // tm):
            @pl.when(t >= na_ref[0])
            def _(t=t):
                for q in range(0, tm, zrows):
                    fill(t * tm + q, zrows)

    def body(r, carry):
        src = x_ref.at[pl.ds(r, 1)]
        p0 = pos_ref[i * td + r]
        p1 = pos_ref[n + i * td + r]
        pltpu.make_async_copy(src, o_hbm.at[pl.ds(p0, 1)], sem).start()
        pltpu.make_async_copy(src, o_hbm.at[pl.ds(p1, 1)], sem).start()
        return carry

    lax.fori_loop(0, td, body, 0, unroll=8)
    for _ in range(2):
        pltpu.make_async_copy(x_ref, o_hbm.at[pl.ds(0, td)], sem).wait()


def _dispatch(x, pos, starts, counts, n_active, *, rows, tm, td):
    n, d = x.shape
    return pl.pallas_call(
        functools.partial(_dispatch_kernel, n=n, tm=tm, td=td),
        grid_spec=pltpu.PrefetchScalarGridSpec(
            num_scalar_prefetch=4,
            grid=(n // td,),
            in_specs=[pl.BlockSpec((td, d), lambda i, *_: (i, 0))],
            out_specs=pl.BlockSpec(memory_space=pl.ANY),
            scratch_shapes=[pltpu.VMEM((64, d), F32), pltpu.SemaphoreType.DMA, pltpu.SemaphoreType.DMA],
        ),
        out_shape=jax.ShapeDtypeStruct((rows, d), F32),
        compiler_params=_cparams(("arbitrary",)),
        name="moe_dispatch",
    )(pos, starts, counts, n_active, x)


def _moe_ffn_kernel(te_ref, na_ref, x_ref, wg_ref, wu_ref, wd_ref, y_ref, xb_ref, acc_ref):
    i = pl.program_id(0)
    j = pl.program_id(1)

    @pl.when(i < na_ref[0])
    def _():
        @pl.when(j == 0)
        def _():
            xb_ref[...] = x_ref[...].astype(BF16)

        x = xb_ref[...]
        a = (_silu(_dot(x, wg_ref[...])) * _dot(x, wu_ref[...])).astype(BF16)
        part = _dot(a, wd_ref[...])

        @pl.when(j == 0)
        def _():
            acc_ref[...] = part

        @pl.when(j > 0)
        def _():
            acc_ref[...] += part

        @pl.when(j == pl.num_programs(1) - 1)
        def _():
            y_ref[...] = acc_ref[...]

    @pl.when(jnp.logical_and(i >= na_ref[0], j == pl.num_programs(1) - 1))
    def _():
        y_ref[...] = jnp.zeros(y_ref.shape, F32)


def _moe_ffn(xs, tile_expert, n_active, wg, wu, wd, *, tm, tf):
    rows, d = xs.shape
    f = wg.shape[2]
    nj = f // tf

    def tile_of(i, na):
        return jnp.minimum(i, na[0] - 1)

    def chunk_of(i, j, na):
        return jnp.where(i < na[0], j, nj - 1)

    return pl.pallas_call(
        _moe_ffn_kernel,
        grid_spec=pltpu.PrefetchScalarGridSpec(
            num_scalar_prefetch=2,
            grid=(rows // tm, nj),
            in_specs=[
                pl.BlockSpec((tm, d), lambda i, j, te, na: (tile_of(i, na), 0)),
                pl.BlockSpec((None, d, tf), lambda i, j, te, na: (te[tile_of(i, na)], 0, chunk_of(i, j, na))),
                pl.BlockSpec((None, d, tf), lambda i, j, te, na: (te[tile_of(i, na)], 0, chunk_of(i, j, na))),
                pl.BlockSpec((None, tf, d), lambda i, j, te, na: (te[tile_of(i, na)], chunk_of(i, j, na), 0)),
            ],
            out_specs=pl.BlockSpec((tm, d), lambda i, j, te, na: (i, 0)),
            scratch_shapes=[pltpu.VMEM((tm, d), BF16), pltpu.VMEM((tm, d), F32)],
        ),
        out_shape=jax.ShapeDtypeStruct((rows, d), F32),
        compiler_params=_cparams(("arbitrary", "arbitrary")),
        name="moe_ffn",
    )(tile_expert, n_active, xs, wg, wu, wd)


def _combine_kernel(pos_ref, h_ref, g_ref, gf_ref, y_hbm, o_ref, buf_ref, sem, *, n, td, final_norm):
    i = pl.program_id(0)
    nsteps = pl.num_programs(0)
    slot = i % 2

    def issue(step, sl):
        def body(r, carry):
            p0 = pos_ref[step * td + r]
            p1 = pos_ref[n + step * td + r]
            pltpu.make_async_copy(y_hbm.at[pl.ds(p0, 1)], buf_ref.at[sl, 0, pl.ds(r, 1)], sem.at[sl]).start()
            pltpu.make_async_copy(y_hbm.at[pl.ds(p1, 1)], buf_ref.at[sl, 1, pl.ds(r, 1)], sem.at[sl]).start()
            return carry

        lax.fori_loop(0, td, body, 0, unroll=8)

    @pl.when(i == 0)
    def _():
        issue(0, 0)

    @pl.when(i + 1 < nsteps)
    def _():
        issue(i + 1, 1 - slot)

    for k in range(2):
        pltpu.make_async_copy(y_hbm.at[pl.ds(0, td)], buf_ref.at[slot, k], sem.at[slot]).wait()

    g = g_ref[...]
    out = h_ref[...] + g[:, 0:1] * buf_ref[slot, 0] + g[:, 1:2] * buf_ref[slot, 1]
    if final_norm:
        out = _rms(out, gf_ref[...])
    o_ref[...] = out


def _combine(h, gates_t, pos, y, gf, *, td, final_norm):
    n, d = h.shape
    return pl.pallas_call(
        functools.partial(_combine_kernel, n=n, td=td, final_norm=final_norm),
        grid_spec=pltpu.PrefetchScalarGridSpec(
            num_scalar_prefetch=1,
            grid=(n // td,),
            in_specs=[
                pl.BlockSpec((td, d), lambda i, *_: (i, 0)),
                pl.BlockSpec((td, 2), lambda i, *_: (i, 0)),
                pl.BlockSpec((1, d), lambda i, *_: (0, 0)),
                pl.BlockSpec(memory_space=pl.ANY),
            ],
            out_specs=pl.BlockSpec((td, d), lambda i, *_: (i, 0)),
            scratch_shapes=[pltpu.VMEM((2, 2, td, d), F32), pltpu.SemaphoreType.DMA((2,))],
        ),
        out_shape=jax.ShapeDtypeStruct((n, d), F32),
        compiler_params=_cparams(("arbitrary",)),
        name="moe_combine",
    )(pos, h, gates_t, gf.reshape(1, d), y)


MIX_TILE = 512
FFN_TM = 1024
FFN_TF = 256
MOE_TM = 512
MOE_TF = 512
ROW_TD = 256


def _moe_rows(n, tm):
    return ((2 * n + N_EXPERTS * (tm - 1)) // tm) * tm


def kernel(x, norm_mix_e, w_in_e, sgu_norm, w_spatial, b_spatial, w_pool, pool_scale, w_out_e, norm_ffn_e, w_gate_d, w_up_d, w_down_d, norm_mix_o, w_pw1, b_pw1, w_dw, b_dw, ln_g, ln_b, w_pw2, b_pw2, norm_ffn_o, w_router, w_gate_x, w_up_x, w_down_x, norm_final):
    b, s, d = x.shape
    n = b * s
    depth = norm_mix_e.shape[0] + norm_mix_o.shape[0]
    rows = _moe_rows(n, MOE_TM)
    h = x
    for layer in range(depth):
        i = layer // 2
        if layer % 2 == 0:
            h = _even_mixer(h, norm_mix_e[i], w_in_e[i], sgu_norm[i], w_spatial[i], b_spatial[i], w_pool[i],
                            pool_scale[i], w_out_e[i], tile=MIX_TILE)
            h = _dense_ffn(h.reshape(n, d), norm_ffn_e[i], w_gate_d[i], w_up_d[i], w_down_d[i],
                           tm=FFN_TM, tf=FFN_TF).reshape(b, s, d)
        else:
            h2, hn, ri, rg = _odd_mixer(h, norm_mix_o[i], w_pw1[i], b_pw1[i], w_dw[i], b_dw[i], ln_g[i], ln_b[i],
                                        w_pw2[i], b_pw2[i], norm_ffn_o[i], w_router[i], tile=MIX_TILE)
            pos, tab = _route_meta(ri, tm=MOE_TM)
            xs = _dispatch(hn, pos, tab[2, :N_EXPERTS], tab[3, :N_EXPERTS], tab[1, :1], rows=rows, tm=MOE_TM, td=ROW_TD)
            ys = _moe_ffn(xs, tab[0, :rows // MOE_TM], tab[1, :1], w_gate_x[i].astype(BF16),
                          w_up_x[i].astype(BF16), w_down_x[i].astype(BF16), tm=MOE_TM, tf=MOE_TF)
            last = layer == depth - 1
            h = _combine(h2.reshape(n, d), rg.T, pos, ys, norm_final, td=ROW_TD,
                         final_norm=last).reshape(b, s, d)
    if depth % 2 == 1:
        raise NotImplementedError("trunk depth must end on an odd (MoE) layer")
    return h
```

```python
import functools

import jax
import jax.numpy as jnp
from jax import lax
from jax.experimental import pallas as pl
from jax.experimental.pallas import tpu as pltpu

F32 = jnp.float32
BF16 = jnp.bfloat16
I32 = jnp.int32

EPS = 1e-6
CHUNK = 128
A_HEADS = 8
POOL_WINDOWS = (2, 4, 8, 16)
POOL_HALO = 16
CONV_K = 31
CONV_HALO = 32
N_EXPERTS = 8
LANES = 128
VMEM_LIMIT = 56 * 1024 * 1024


def _cparams(sem):
    return pltpu.CompilerParams(dimension_semantics=sem, vmem_limit_bytes=VMEM_LIMIT)


def _rms(x, g):
    return x * lax.rsqrt(jnp.mean(x * x, axis=-1, keepdims=True) + EPS) * g


def _gelu_tanh(x):
    c = 0.7978845608028654
    return 0.5 * x * (1.0 + jnp.tanh(c * (x + 0.044715 * (x * x * x))))


def _silu(x):
    return x * jax.nn.sigmoid(x)


def _dot(a, b):
    return jnp.dot(a, b, preferred_element_type=F32)


def _even_mixer_kernel(h_ref, gn_ref, win_ref, gv_ref, ws_ref, bst_ref, wp_ref, ps_ref, wout_ref,
                       o_ref, zs_ref, cat_ref, *, tile):
    s = pl.program_id(1)
    d = h_ref.shape[-1]
    aw = d // 2
    nc = tile // CHUNK
    h = h_ref[0]
    hn = _rms(h, gn_ref[...]).astype(BF16)
    proj = _dot(hn, win_ref[...])
    u = _gelu_tanh(proj[:, :aw])
    v = _rms(_gelu_tanh(proj[:, aw:2 * aw]), gv_ref[...])
    z = proj[:, 2 * aw:]

    row = lax.broadcasted_iota(I32, (CHUNK, CHUNK), 0)
    col = lax.broadcasted_iota(I32, (CHUNK, CHUNK), 1)
    causal = row >= col
    lane = lax.broadcasted_iota(I32, (CHUNK, LANES), 1)
    lo = lane < (LANES // 2)
    for j in range(aw // LANES):
        w_a = jnp.where(causal, ws_ref[2 * j], 0).astype(BF16)
        w_b = jnp.where(causal, ws_ref[2 * j + 1], 0).astype(BF16)
        lhs = jnp.concatenate([w_a, w_b], axis=1)
        cols = []
        for c in range(nc):
            vb = v[c * CHUNK:(c + 1) * CHUNK, j * LANES:(j + 1) * LANES]
            cols.append(jnp.concatenate([jnp.where(lo, vb, 0.0), jnp.where(lo, 0.0, vb)], axis=0))
        rhs = jnp.concatenate(cols, axis=1).astype(BF16)
        mixed = _dot(lhs, rhs)
        bias = jnp.where(lo, bst_ref[:, 2 * j:2 * j + 1], bst_ref[:, 2 * j + 1:2 * j + 2])
        for c in range(nc):
            ub = u[c * CHUNK:(c + 1) * CHUNK, j * LANES:(j + 1) * LANES]
            a_out = ub * (mixed[:, c * LANES:(c + 1) * LANES] + bias)
            cat_ref[c * CHUNK:(c + 1) * CHUNK, j * LANES:(j + 1) * LANES] = a_out.astype(BF16)

    @pl.when(s == 0)
    def _():
        zs_ref[0:POOL_HALO, :] = jnp.zeros((POOL_HALO, aw), F32)

    @pl.when(s > 0)
    def _():
        zs_ref[0:POOL_HALO, :] = zs_ref[tile:tile + POOL_HALO, :]

    zs_ref[POOL_HALO:POOL_HALO + tile, :] = z
    pos = s * tile + 1 + lax.broadcasted_iota(I32, (tile, LANES), 0)
    for g, win in enumerate(POOL_WINDOWS):
        gs = slice(g * LANES, (g + 1) * LANES)
        acc = zs_ref[POOL_HALO:POOL_HALO + tile, gs]
        for i in range(1, win):
            acc = acc + zs_ref[POOL_HALO - i:POOL_HALO - i + tile, gs]
        cnt = jnp.minimum(pos, win).astype(F32)
        mixed = acc / cnt - z[:, gs]
        y = _dot(mixed.astype(BF16), wp_ref[g]) * ps_ref[:, gs]
        cat_ref[:, aw + g * LANES:aw + (g + 1) * LANES] = y.astype(BF16)

    o_ref[0] = h + _dot(cat_ref[...], wout_ref[...])


def _even_mixer(h, gn, w_in, gv, ws, bs, wp, ps, w_out, *, tile):
    b, s, d = h.shape
    aw = d // 2
    const2 = lambda *_: (0, 0)
    const3 = lambda *_: (0, 0, 0)
    return pl.pallas_call(
        functools.partial(_even_mixer_kernel, tile=tile),
        grid=(b, s // tile),
        in_specs=[
            pl.BlockSpec((1, tile, d), lambda i, j: (i, j, 0)),
            pl.BlockSpec((1, d), const2),
            pl.BlockSpec((d, 3 * aw), const2),
            pl.BlockSpec((1, aw), const2),
            pl.BlockSpec((A_HEADS, CHUNK, CHUNK), const3),
            pl.BlockSpec((CHUNK, A_HEADS), const2),
            pl.BlockSpec((len(POOL_WINDOWS), LANES, LANES), const3),
            pl.BlockSpec((1, aw), const2),
            pl.BlockSpec((d, d), const2),
        ],
        out_specs=pl.BlockSpec((1, tile, d), lambda i, j: (i, j, 0)),
        out_shape=jax.ShapeDtypeStruct(h.shape, F32),
        scratch_shapes=[pltpu.VMEM((tile + POOL_HALO, aw), F32), pltpu.VMEM((tile, d), BF16)],
        compiler_params=_cparams(("arbitrary", "arbitrary")),
        name="even_mixer",
    )(h, gn.reshape(1, d), w_in.astype(BF16), gv.reshape(1, aw), ws, bs.T, wp.astype(BF16),
      ps.reshape(1, aw), w_out.astype(BF16))


def _swiglu_hidden(x, wg_ref, wu_ref, a_ref, *, sub):
    for c in range(a_ref.shape[1] // sub):
        cs = slice(c * sub, (c + 1) * sub)
        a_ref[:, cs] = (_silu(_dot(x, wg_ref[:, cs])) * _dot(x, wu_ref[:, cs])).astype(BF16)


def _dense_ffn_kernel(h_ref, gn_ref, wg_ref, wu_ref, wd_ref, o_ref, a_ref, *, sub):
    h = h_ref[...]
    x = _rms(h, gn_ref[...]).astype(BF16)
    _swiglu_hidden(x, wg_ref, wu_ref, a_ref, sub=sub)
    o_ref[...] = h + _dot(a_ref[...], wd_ref[...])


def _dense_ffn(h, gn, wg, wu, wd, *, tm, sub):
    n, d = h.shape
    f = wg.shape[1]
    const2 = lambda i: (0, 0)
    return pl.pallas_call(
        functools.partial(_dense_ffn_kernel, sub=sub),
        grid=(n // tm,),
        in_specs=[
            pl.BlockSpec((tm, d), lambda i: (i, 0)),
            pl.BlockSpec((1, d), const2),
            pl.BlockSpec((d, f), const2),
            pl.BlockSpec((d, f), const2),
            pl.BlockSpec((f, d), const2),
        ],
        out_specs=pl.BlockSpec((tm, d), lambda i: (i, 0)),
        out_shape=jax.ShapeDtypeStruct((n, d), F32),
        scratch_shapes=[pltpu.VMEM((tm, f), BF16)],
        compiler_params=_cparams(("arbitrary",)),
        name="dense_ffn",
    )(h, gn.reshape(1, d), wg.astype(BF16), wu.astype(BF16), wd.astype(BF16))


def _split3(x):
    hi = x.astype(BF16)
    lo = (x - hi.astype(F32)).astype(BF16)
    return hi, lo


def _odd_mixer_kernel(h_ref, gn_ref, w1_ref, b1_ref, wdw_ref, bdw_ref, lg_ref, lb_ref, w2_ref, b2_ref,
                      gf_ref, wr_ref, o_ref, hn_ref, ri_ref, rg_ref, xs_ref, ys_ref, *, tile):
    s = pl.program_id(1)
    d = h_ref.shape[-1]
    ncb = d // LANES
    h = h_ref[0]
    hn = _rms(h, gn_ref[...]).astype(BF16)
    a = _dot(hn, w1_ref[...]) + b1_ref[...]
    x = a[:, :d] * jax.nn.sigmoid(a[:, d:])

    @pl.when(s == 0)
    def _():
        xs_ref[:, 0:CONV_HALO, :] = jnp.zeros((ncb, CONV_HALO, LANES), F32)

    @pl.when(s > 0)
    def _():
        xs_ref[:, 0:CONV_HALO, :] = xs_ref[:, tile:tile + CONV_HALO, :]

    for cb in range(ncb):
        xs_ref[cb, CONV_HALO:CONV_HALO + tile, :] = x[:, cb * LANES:(cb + 1) * LANES]

    rb = 128
    base = CONV_HALO - (CONV_K - 1)

    def conv_block(cb, carry):
        for r0 in range(0, tile, rb):
            acc = jnp.zeros((rb, LANES), F32)
            for k in range(CONV_K):
                acc = acc + wdw_ref[cb, k:k + 1, :] * xs_ref[cb, r0 + base + k:r0 + base + k + rb, :]
            ys_ref[cb, r0:r0 + rb, :] = acc
        return carry

    lax.fori_loop(0, ncb, conv_block, 0)
    y = jnp.concatenate([ys_ref[cb] for cb in range(ncb)], axis=1) + bdw_ref[...]

    mu = jnp.mean(y, axis=-1, keepdims=True)
    yc = y - mu
    var = jnp.mean(yc * yc, axis=-1, keepdims=True)
    yn = _silu(yc * lax.rsqrt(var + EPS) * lg_ref[...] + lb_ref[...])
    h2 = h + _dot(yn.astype(BF16), w2_ref[...]) + b2_ref[...]
    o_ref[0] = h2

    hf = _rms(h2, gf_ref[...])
    hn_ref[...] = hf.reshape(tile, d // LANES, LANES)
    x_hi, x_lo = _split3(hf)
    w_hi, w_lo = _split3(wr_ref[...])
    nt = (((1,), (1,)), ((), ()))
    logits = (lax.dot_general(w_hi, x_hi, nt, preferred_element_type=F32)
              + lax.dot_general(w_lo, x_hi, nt, preferred_element_type=F32)
              + lax.dot_general(w_hi, x_lo, nt, preferred_element_type=F32))
    eidx = lax.broadcasted_iota(I32, logits.shape, 0).astype(F32)
    m1 = jnp.max(logits, axis=0, keepdims=True)
    i1 = jnp.min(jnp.where(logits == m1, eidx, float(N_EXPERTS)), axis=0, keepdims=True)
    rest = jnp.where(eidx == i1, -jnp.inf, logits)
    m2 = jnp.max(rest, axis=0, keepdims=True)
    i2 = jnp.min(jnp.where(rest == m2, eidx, float(N_EXPERTS)), axis=0, keepdims=True)
    e2 = jnp.exp(m2 - m1)
    den = 1.0 + e2
    ri_ref[...] = jnp.concatenate([i1, i2], axis=0).astype(I32)
    rg_ref[...] = jnp.concatenate([1.0 / den, e2 / den], axis=0)


def _odd_mixer(h, gn, w1, b1, wdw, bdw, lg, lb, w2, b2, gf, wr, *, tile):
    b, s, d = h.shape
    n = b * s
    ncb = d // LANES
    nst = s // tile
    const2 = lambda *_: (0, 0)
    const3 = lambda *_: (0, 0, 0)
    wdw_b = jnp.transpose(wdw.reshape(CONV_K, ncb, LANES), (1, 0, 2))
    return pl.pallas_call(
        functools.partial(_odd_mixer_kernel, tile=tile),
        grid=(b, nst),
        in_specs=[
            pl.BlockSpec((1, tile, d), lambda i, j: (i, j, 0)),
            pl.BlockSpec((1, d), const2),
            pl.BlockSpec((d, 2 * d), const2),
            pl.BlockSpec((1, 2 * d), const2),
            pl.BlockSpec((ncb, CONV_K, LANES), const3),
            pl.BlockSpec((1, d), const2),
            pl.BlockSpec((1, d), const2),
            pl.BlockSpec((1, d), const2),
            pl.BlockSpec((d, d), const2),
            pl.BlockSpec((1, d), const2),
            pl.BlockSpec((1, d), const2),
            pl.BlockSpec((N_EXPERTS, d), const2),
        ],
        out_specs=[
            pl.BlockSpec((1, tile, d), lambda i, j: (i, j, 0)),
            pl.BlockSpec((tile, ncb, LANES), lambda i, j: (i * nst + j, 0, 0)),
            pl.BlockSpec((2, tile), lambda i, j: (0, i * nst + j)),
            pl.BlockSpec((2, tile), lambda i, j: (0, i * nst + j)),
        ],
        out_shape=[
            jax.ShapeDtypeStruct(h.shape, F32),
            jax.ShapeDtypeStruct((n, ncb, LANES), F32),
            jax.ShapeDtypeStruct((2, n), I32),
            jax.ShapeDtypeStruct((2, n), F32),
        ],
        scratch_shapes=[pltpu.VMEM((ncb, tile + CONV_HALO, LANES), F32), pltpu.VMEM((ncb, tile, LANES), F32)],
        compiler_params=_cparams(("arbitrary", "arbitrary")),
        name="odd_mixer",
    )(h, gn.reshape(1, d), w1.astype(BF16), b1.reshape(1, 2 * d), wdw_b, bdw.reshape(1, d),
      lg.reshape(1, d), lb.reshape(1, d), w2.astype(BF16), b2.reshape(1, d), gf.reshape(1, d), wr.T)


def _route_meta_kernel(ri_ref, pos_ref, tab_ref, *, tm):
    ri = ri_ref[...]
    i0, i1 = ri[0], ri[1]
    nb = i0.shape[0]
    shift = tm.bit_length() - 1
    r = lax.broadcasted_iota(I32, (LANES, LANES), 0)
    c = lax.broadcasted_iota(I32, (LANES, LANES), 1)
    upper = (r < c).astype(BF16)
    rb = lax.broadcasted_iota(I32, (nb, nb), 0)
    cb = lax.broadcasted_iota(I32, (nb, nb), 1)
    lower = (cb < rb).astype(BF16)
    lane = lax.broadcasted_iota(I32, (1, LANES), 1)
    pos0 = jnp.zeros(i0.shape, I32)
    pos1 = jnp.zeros(i0.shape, I32)
    goff = jnp.zeros((1, 1), I32)
    tile_expert = jnp.zeros((1, LANES), I32)
    pad_lo = jnp.zeros((1, LANES), I32)
    pad_hi = jnp.zeros((1, LANES), I32)
    for e in range(N_EXPERTS):
        m = jnp.logical_or(i0 == e, i1 == e).astype(F32)
        within = _dot(m.astype(BF16), upper)
        tot = jnp.broadcast_to(jnp.sum(m, axis=1, keepdims=True), (nb, LANES))
        blk = _dot(lower, tot.astype(BF16))
        cnt = jnp.sum(tot[:, 0:1], axis=0, keepdims=True).astype(I32)
        dest = goff + (blk + within).astype(I32)
        pos0 = jnp.where(i0 == e, dest, pos0)
        pos1 = jnp.where(i1 == e, dest, pos1)
        pad_lo = jnp.where(lane == e, goff + cnt, pad_lo)
        goff = goff + (((cnt + (tm - 1)) >> shift) << shift)
        pad_hi = jnp.where(lane == e, goff, pad_hi)
        tile_expert = tile_expert + ((lane << shift) >= goff).astype(I32)
    pos_ref[0] = pos0
    pos_ref[1] = pos1
    tab_ref[0:1, :] = jnp.minimum(tile_expert, N_EXPERTS - 1)
    tab_ref[1:2, :] = jnp.broadcast_to(goff >> shift, (1, LANES))
    tab_ref[2:3, :] = pad_lo
    tab_ref[3:4, :] = pad_hi
    tab_ref[4:8, :] = jnp.zeros((4, LANES), I32)


def _route_meta(ri, *, tm):
    n = ri.shape[1]
    nb = n // LANES
    pos, tab = pl.pallas_call(
        functools.partial(_route_meta_kernel, tm=tm),
        out_shape=[jax.ShapeDtypeStruct((2, nb, LANES), I32), jax.ShapeDtypeStruct((8, LANES), I32)],
        compiler_params=pltpu.CompilerParams(vmem_limit_bytes=VMEM_LIMIT),
        name="route_meta",
    )(ri.reshape(2, nb, LANES))
    return pos.reshape(2 * n), tab


def _route_inv_kernel(pos_ref, lo_ref, hi_ref, code_ref, *, n, tm):
    def pad(p, carry):
        code_ref[p] = 2 * n + (p & (2 * tm - 1))
        return carry

    for e in range(N_EXPERTS):
        lax.fori_loop(lo_ref[e], hi_ref[e], pad, 0)
    lax.fori_loop(hi_ref[N_EXPERTS - 1], code_ref.shape[0], pad, 0)

    def body(t, carry):
        code_ref[pos_ref[t]] = 2 * t
        code_ref[pos_ref[n + t]] = 2 * t + 1
        return carry

    lax.fori_loop(0, n, body, 0, unroll=8)


def _route_inv(pos, pad_lo, pad_hi, *, n, rows, tm):
    smem = pl.BlockSpec(memory_space=pltpu.SMEM)
    return pl.pallas_call(
        functools.partial(_route_inv_kernel, n=n, tm=tm),
        in_specs=[smem, smem, smem],
        out_specs=smem,
        out_shape=jax.ShapeDtypeStruct((rows,), I32),
        name="route_inv",
    )(pos, pad_lo, pad_hi)


def _moe_ffn_kernel(te_ref, na_ref, code_ref, hn_hbm, wg_ref, wu_ref, wd_ref, o_hbm,
                    xbuf, ybuf, xb_ref, a_ref, acc_ref, gsem, ssem, *, n, tm, nj, sub):
    i = pl.program_id(0)
    j = pl.program_id(1)
    nt = pl.num_programs(0)
    na = na_ref[0]
    slot = i % 2
    d = xb_ref.shape[1]

    def gather(tile, sl):
        def body(r, carry):
            code = code_ref[tile * tm + r]
            tok = jnp.where(code < 2 * n, code >> 1, 0)
            pltpu.make_async_copy(hn_hbm.at[tok], xbuf.at[sl, r], gsem.at[sl]).start()
            return carry

        lax.fori_loop(0, tm, body, 0, unroll=8)

    def scatter(tile, sl):
        def body(r, carry):
            code = code_ref[tile * tm + r]
            pltpu.make_async_copy(ybuf.at[sl, r], o_hbm.at[code], ssem.at[sl]).start()
            return carry

        lax.fori_loop(0, tm, body, 0, unroll=8)

    def wait_gather(sl):
        pltpu.make_async_copy(hn_hbm.at[pl.ds(0, tm)], xbuf.at[sl], gsem.at[sl]).wait()

    def wait_scatter(sl):
        pltpu.make_async_copy(ybuf.at[sl], o_hbm.at[pl.ds(0, tm)], ssem.at[sl]).wait()

    @pl.when(jnp.logical_and(i == 0, j == 0))
    def _():
        ybuf[0] = jnp.zeros(ybuf.shape[1:], F32)
        for q in range(2):
            cp = pltpu.make_async_copy(ybuf.at[0], o_hbm.at[pl.ds(2 * n + q * tm, tm)], ssem.at[0])
            cp.start()
            cp.wait()
        gather(0, 0)

    @pl.when(i < na)
    def _():
        @pl.when(j == 0)
        def _():
            wait_gather(slot)
            xb_ref[...] = xbuf[slot].reshape(tm, d).astype(BF16)

            @pl.when(i + 1 < na)
            def _():
                gather(i + 1, 1 - slot)

        _swiglu_hidden(xb_ref[...], wg_ref, wu_ref, a_ref, sub=sub)
        part = _dot(a_ref[...], wd_ref[...])

        if nj > 1:
            @pl.when(j == 0)
            def _():
                acc_ref[...] = part

            @pl.when(jnp.logical_and(j > 0, j < nj - 1))
            def _():
                acc_ref[...] += part

        @pl.when(j == nj - 1)
        def _():
            y = part if nj == 1 else acc_ref[...] + part

            @pl.when(i >= 2)
            def _():
                wait_scatter(slot)

            ybuf[slot] = y.reshape(tm, d // LANES, LANES)
            scatter(i, slot)

    @pl.when(jnp.logical_and(i == nt - 1, j == nj - 1))
    def _():
        @pl.when(na >= 2)
        def _():
            wait_scatter(na % 2)

        wait_scatter((na + 1) % 2)


def _moe_ffn(hn, tile_expert, n_active, code, wg, wu, wd, *, tm, tf, sub):
    n, ncb, _ = hn.shape
    d = ncb * LANES
    rows = code.shape[0]
    f = wg.shape[2]
    nj = f // tf

    def tile_of(i, na):
        return jnp.minimum(i, na[0] - 1)

    def chunk_of(i, j, na):
        return jnp.where(i < na[0], j, nj - 1)

    return pl.pallas_call(
        functools.partial(_moe_ffn_kernel, n=n, tm=tm, nj=nj, sub=sub),
        grid_spec=pltpu.PrefetchScalarGridSpec(
            num_scalar_prefetch=3,
            grid=(rows // tm, nj),
            in_specs=[
                pl.BlockSpec(memory_space=pl.ANY),
                pl.BlockSpec((None, d, tf), lambda i, j, te, na, cd: (te[tile_of(i, na)], 0, chunk_of(i, j, na))),
                pl.BlockSpec((None, d, tf), lambda i, j, te, na, cd: (te[tile_of(i, na)], 0, chunk_of(i, j, na))),
                pl.BlockSpec((None, tf, d), lambda i, j, te, na, cd: (te[tile_of(i, na)], chunk_of(i, j, na), 0)),
            ],
            out_specs=pl.BlockSpec(memory_space=pl.ANY),
            scratch_shapes=[
                pltpu.VMEM((2, tm, ncb, LANES), F32),
                pltpu.VMEM((2, tm, ncb, LANES), F32),
                pltpu.VMEM((tm, d), BF16),
                pltpu.VMEM((tm, tf), BF16),
                pltpu.VMEM((tm, d), F32),
                pltpu.SemaphoreType.DMA((2,)),
                pltpu.SemaphoreType.DMA((2,)),
            ],
        ),
        out_shape=jax.ShapeDtypeStruct((2 * n + 2 * tm, ncb, LANES), F32),
        compiler_params=_cparams(("arbitrary", "arbitrary")),
        name="moe_ffn",
    )(tile_expert, n_active, code, hn, wg, wu, wd)


def _moe_add_kernel(h_ref, g_ref, gf_ref, y_ref, o_ref, *, final_norm):
    td, d = h_ref.shape
    g = g_ref[...]
    out = h_ref[...] + g[:, 0:1] * y_ref[:, 0].reshape(td, d) + g[:, 1:2] * y_ref[:, 1].reshape(td, d)
    if final_norm:
        out = _rms(out, gf_ref[...])
    o_ref[...] = out


def _moe_add(h, gates_t, y, gf, *, td, final_norm):
    n, d = h.shape
    ncb = d // LANES
    return pl.pallas_call(
        functools.partial(_moe_add_kernel, final_norm=final_norm),
        grid=(n // td,),
        in_specs=[
            pl.BlockSpec((td, d), lambda i: (i, 0)),
            pl.BlockSpec((td, 2), lambda i: (i, 0)),
            pl.BlockSpec((1, d), lambda i: (0, 0)),
            pl.BlockSpec((td, 2, ncb, LANES), lambda i: (i, 0, 0, 0)),
        ],
        out_specs=pl.BlockSpec((td, d), lambda i: (i, 0)),
        out_shape=jax.ShapeDtypeStruct((n, d), F32),
        compiler_params=_cparams(("arbitrary",)),
        name="moe_add",
    )(h, gates_t, gf.reshape(1, d), y.reshape(y.shape[0] // 2, 2, ncb, LANES))


MIX_TILE = 512
FFN_TM = 512
FFN_SUB = 256
MOE_TM = 512
MOE_TF = 1792
ADD_TD = 512


def _moe_rows(n, tm):
    return ((2 * n + N_EXPERTS * (tm - 1)) // tm) * tm


def kernel(x, norm_mix_e, w_in_e, sgu_norm, w_spatial, b_spatial, w_pool, pool_scale, w_out_e, norm_ffn_e, w_gate_d, w_up_d, w_down_d, norm_mix_o, w_pw1, b_pw1, w_dw, b_dw, ln_g, ln_b, w_pw2, b_pw2, norm_ffn_o, w_router, w_gate_x, w_up_x, w_down_x, norm_final):
    b, s, d = x.shape
    n = b * s
    depth = norm_mix_e.shape[0] + norm_mix_o.shape[0]
    if depth % 2 == 1:
        raise NotImplementedError("trunk depth must end on an odd (MoE) layer")
    rows = _moe_rows(n, MOE_TM)
    h = x
    for layer in range(depth):
        i = layer // 2
        if layer % 2 == 0:
            h = _even_mixer(h, norm_mix_e[i], w_in_e[i], sgu_norm[i], w_spatial[i], b_spatial[i], w_pool[i],
                            pool_scale[i], w_out_e[i], tile=MIX_TILE)
            h = _dense_ffn(h.reshape(n, d), norm_ffn_e[i], w_gate_d[i], w_up_d[i], w_down_d[i],
                           tm=FFN_TM, sub=FFN_SUB).reshape(b, s, d)
        else:
            h2, hn, ri, rg = _odd_mixer(h, norm_mix_o[i], w_pw1[i], b_pw1[i], w_dw[i], b_dw[i], ln_g[i], ln_b[i],
                                        w_pw2[i], b_pw2[i], norm_ffn_o[i], w_router[i], tile=MIX_TILE)
            pos, tab = _route_meta(ri, tm=MOE_TM)
            code = _route_inv(pos, tab[2, :N_EXPERTS], tab[3, :N_EXPERTS], n=n, rows=rows, tm=MOE_TM)
            ys = _moe_ffn(hn, tab[0, :rows // MOE_TM], tab[1, :1], code, w_gate_x[i].astype(BF16),
                          w_up_x[i].astype(BF16), w_down_x[i].astype(BF16), tm=MOE_TM, tf=MOE_TF, sub=FFN_SUB)
            h = _moe_add(h2.reshape(n, d), rg.T, ys, norm_final, td=ADD_TD,
                         final_norm=layer == depth - 1).reshape(b, s, d)
    return h
```

```python
import functools

import jax
import jax.numpy as jnp
from jax import lax
from jax.experimental import pallas as pl
from jax.experimental.pallas import tpu as pltpu

F32 = jnp.float32
BF16 = jnp.bfloat16
I32 = jnp.int32

EPS = 1e-6
CHUNK = 128
A_HEADS = 8
POOL_WINDOWS = (2, 4, 8, 16)
POOL_HALO = 16
CONV_K = 31
CONV_HALO = 32
N_EXPERTS = 8
SPILL_TILES = 3
LANES = 128
VMEM_LIMIT = 56 * 1024 * 1024


def _cparams(sem):
    return pltpu.CompilerParams(dimension_semantics=sem, vmem_limit_bytes=VMEM_LIMIT)


def _rms(x, g):
    return x * lax.rsqrt(jnp.mean(x * x, axis=-1, keepdims=True) + EPS) * g


def _gelu_tanh(x):
    c = 0.7978845608028654
    return 0.5 * x * (1.0 + jnp.tanh(c * (x + 0.044715 * (x * x * x))))


def _silu(x):
    return x * jax.nn.sigmoid(x)


def _dot(a, b):
    return jnp.dot(a, b, preferred_element_type=F32)


def _moe_combine(h, g_ref, y_ref):
    g = g_ref[...]
    return h + g[:, 0:1] * y_ref[:, 0].reshape(h.shape) + g[:, 1:2] * y_ref[:, 1].reshape(h.shape)


def _even_mixer_kernel(*refs, tile, combine):
    if combine:
        h_ref, g_ref, y_ref = refs[:3]
        refs = refs[3:]
    else:
        h_ref = refs[0]
        refs = refs[1:]
    gn_ref, win_ref, gv_ref, ws_ref, bst_ref, wp_ref, ps_ref, wout_ref, o_ref, zs_ref, cat_ref = refs
    s = pl.program_id(1)
    d = h_ref.shape[-1]
    aw = d // 2
    nc = tile // CHUNK
    h = h_ref[0]
    if combine:
        h = _moe_combine(h, g_ref, y_ref)
    hn = _rms(h, gn_ref[...]).astype(BF16)
    proj = _dot(hn, win_ref[...])
    u = _gelu_tanh(proj[:, :aw])
    v = _rms(_gelu_tanh(proj[:, aw:2 * aw]), gv_ref[...])
    z = proj[:, 2 * aw:]

    row = lax.broadcasted_iota(I32, (CHUNK, CHUNK), 0)
    col = lax.broadcasted_iota(I32, (CHUNK, CHUNK), 1)
    causal = row >= col
    lane = lax.broadcasted_iota(I32, (CHUNK, LANES), 1)
    lo = lane < (LANES // 2)
    for j in range(aw // LANES):
        w_a = jnp.where(causal, ws_ref[2 * j], 0).astype(BF16)
        w_b = jnp.where(causal, ws_ref[2 * j + 1], 0).astype(BF16)
        lhs = jnp.concatenate([w_a, w_b], axis=1)
        cols = []
        for c in range(nc):
            vb = v[c * CHUNK:(c + 1) * CHUNK, j * LANES:(j + 1) * LANES]
            cols.append(jnp.concatenate([jnp.where(lo, vb, 0.0), jnp.where(lo, 0.0, vb)], axis=0))
        rhs = jnp.concatenate(cols, axis=1).astype(BF16)
        mixed = _dot(lhs, rhs)
        bias = jnp.where(lo, bst_ref[:, 2 * j:2 * j + 1], bst_ref[:, 2 * j + 1:2 * j + 2])
        for c in range(nc):
            ub = u[c * CHUNK:(c + 1) * CHUNK, j * LANES:(j + 1) * LANES]
            a_out = ub * (mixed[:, c * LANES:(c + 1) * LANES] + bias)
            cat_ref[c * CHUNK:(c + 1) * CHUNK, j * LANES:(j + 1) * LANES] = a_out.astype(BF16)

    @pl.when(s == 0)
    def _():
        zs_ref[0:POOL_HALO, :] = jnp.zeros((POOL_HALO, aw), F32)

    @pl.when(s > 0)
    def _():
        zs_ref[0:POOL_HALO, :] = zs_ref[tile:tile + POOL_HALO, :]

    zs_ref[POOL_HALO:POOL_HALO + tile, :] = z
    pos = s * tile + 1 + lax.broadcasted_iota(I32, (tile, LANES), 0)
    for g, win in enumerate(POOL_WINDOWS):
        gs = slice(g * LANES, (g + 1) * LANES)
        acc = zs_ref[POOL_HALO:POOL_HALO + tile, gs]
        for i in range(1, win):
            acc = acc + zs_ref[POOL_HALO - i:POOL_HALO - i + tile, gs]
        cnt = jnp.minimum(pos, win).astype(F32)
        mixed = acc / cnt - z[:, gs]
        y = _dot(mixed.astype(BF16), wp_ref[g]) * ps_ref[:, gs]
        cat_ref[:, aw + g * LANES:aw + (g + 1) * LANES] = y.astype(BF16)

    o_ref[0] = h + _dot(cat_ref[...], wout_ref[...])


def _even_mixer(h, gn, w_in, gv, ws, bs, wp, ps, w_out, *, tile, moe=None):
    b, s, d = h.shape
    aw = d // 2
    nst = s // tile
    ncb = d // LANES
    const2 = lambda *_: (0, 0)
    const3 = lambda *_: (0, 0, 0)
    moe_specs, moe_args = [], []
    if moe is not None:
        gates_t, y = moe
        moe_specs = [pl.BlockSpec((tile, 2), lambda i, j: (i * nst + j, 0)),
                     pl.BlockSpec((tile, 2, ncb, LANES), lambda i, j: (i * nst + j, 0, 0, 0))]
        moe_args = [gates_t, y.reshape(y.shape[0] // 2, 2, ncb, LANES)]
    return pl.pallas_call(
        functools.partial(_even_mixer_kernel, tile=tile, combine=moe is not None),
        grid=(b, nst),
        in_specs=[
            pl.BlockSpec((1, tile, d), lambda i, j: (i, j, 0)),
            *moe_specs,
            pl.BlockSpec((1, d), const2),
            pl.BlockSpec((d, 3 * aw), const2),
            pl.BlockSpec((1, aw), const2),
            pl.BlockSpec((A_HEADS, CHUNK, CHUNK), const3),
            pl.BlockSpec((CHUNK, A_HEADS), const2),
            pl.BlockSpec((len(POOL_WINDOWS), LANES, LANES), const3),
            pl.BlockSpec((1, aw), const2),
            pl.BlockSpec((d, d), const2),
        ],
        out_specs=pl.BlockSpec((1, tile, d), lambda i, j: (i, j, 0)),
        out_shape=jax.ShapeDtypeStruct(h.shape, F32),
        scratch_shapes=[pltpu.VMEM((tile + POOL_HALO, aw), F32), pltpu.VMEM((tile, d), BF16)],
        compiler_params=_cparams(("arbitrary", "arbitrary")),
        name="even_mixer",
    )(h, *moe_args, gn.reshape(1, d), w_in.astype(BF16), gv.reshape(1, aw), ws, bs.T, wp.astype(BF16),
      ps.reshape(1, aw), w_out.astype(BF16))


def _swiglu_hidden(x, wg_ref, wu_ref, a_ref, *, sub):
    for c in range(a_ref.shape[1] // sub):
        cs = slice(c * sub, (c + 1) * sub)
        a_ref[:, cs] = (_silu(_dot(x, wg_ref[:, cs])) * _dot(x, wu_ref[:, cs])).astype(BF16)


def _dense_ffn_kernel(h_ref, gn_ref, wg_ref, wu_ref, wd_ref, o_ref, a_ref, *, sub):
    h = h_ref[...]
    x = _rms(h, gn_ref[...]).astype(BF16)
    _swiglu_hidden(x, wg_ref, wu_ref, a_ref, sub=sub)
    o_ref[...] = h + _dot(a_ref[...], wd_ref[...])


def _dense_ffn(h, gn, wg, wu, wd, *, tm, sub):
    n, d = h.shape
    f = wg.shape[1]
    const2 = lambda i: (0, 0)
    return pl.pallas_call(
        functools.partial(_dense_ffn_kernel, sub=sub),
        grid=(n // tm,),
        in_specs=[
            pl.BlockSpec((tm, d), lambda i: (i, 0)),
            pl.BlockSpec((1, d), const2),
            pl.BlockSpec((d, f), const2),
            pl.BlockSpec((d, f), const2),
            pl.BlockSpec((f, d), const2),
        ],
        out_specs=pl.BlockSpec((tm, d), lambda i: (i, 0)),
        out_shape=jax.ShapeDtypeStruct((n, d), F32),
        scratch_shapes=[pltpu.VMEM((tm, f), BF16)],
        compiler_params=_cparams(("arbitrary",)),
        name="dense_ffn",
    )(h, gn.reshape(1, d), wg.astype(BF16), wu.astype(BF16), wd.astype(BF16))


def _split3(x):
    hi = x.astype(BF16)
    lo = (x - hi.astype(F32)).astype(BF16)
    return hi, lo


def _odd_mixer_kernel(h_ref, gn_ref, w1_ref, b1_ref, wdw_ref, bdw_ref, lg_ref, lb_ref, w2_ref, b2_ref,
                      gf_ref, wr_ref, o_ref, hn_ref, ri_ref, rg_ref, xs_ref, ys_ref, *, tile):
    s = pl.program_id(1)
    d = h_ref.shape[-1]
    ncb = d // LANES
    h = h_ref[0]
    hn = _rms(h, gn_ref[...]).astype(BF16)
    a = _dot(hn, w1_ref[...]) + b1_ref[...]
    x = a[:, :d] * jax.nn.sigmoid(a[:, d:])

    @pl.when(s == 0)
    def _():
        xs_ref[:, 0:CONV_HALO, :] = jnp.zeros((ncb, CONV_HALO, LANES), F32)

    @pl.when(s > 0)
    def _():
        xs_ref[:, 0:CONV_HALO, :] = xs_ref[:, tile:tile + CONV_HALO, :]

    for cb in range(ncb):
        xs_ref[cb, CONV_HALO:CONV_HALO + tile, :] = x[:, cb * LANES:(cb + 1) * LANES]

    rb = 128
    base = CONV_HALO - (CONV_K - 1)

    def conv_block(cb, carry):
        for r0 in range(0, tile, rb):
            acc = jnp.zeros((rb, LANES), F32)
            for k in range(CONV_K):
                acc = acc + wdw_ref[cb, k:k + 1, :] * xs_ref[cb, r0 + base + k:r0 + base + k + rb, :]
            ys_ref[cb, r0:r0 + rb, :] = acc
        return carry

    lax.fori_loop(0, ncb, conv_block, 0)
    y = jnp.concatenate([ys_ref[cb] for cb in range(ncb)], axis=1) + bdw_ref[...]

    mu = jnp.mean(y, axis=-1, keepdims=True)
    yc = y - mu
    var = jnp.mean(yc * yc, axis=-1, keepdims=True)
    yn = _silu(yc * lax.rsqrt(var + EPS) * lg_ref[...] + lb_ref[...])
    h2 = h + _dot(yn.astype(BF16), w2_ref[...]) + b2_ref[...]
    o_ref[0] = h2

    hf = _rms(h2, gf_ref[...])
    hn_ref[...] = hf.reshape(tile, d // LANES, LANES)
    x_hi, x_lo = _split3(hf)
    w_hi, w_lo = _split3(wr_ref[...])
    nt = (((1,), (1,)), ((), ()))
    logits = (lax.dot_general(w_hi, x_hi, nt, preferred_element_type=F32)
              + lax.dot_general(w_lo, x_hi, nt, preferred_element_type=F32)
              + lax.dot_general(w_hi, x_lo, nt, preferred_element_type=F32))
    eidx = lax.broadcasted_iota(I32, logits.shape, 0).astype(F32)
    m1 = jnp.max(logits, axis=0, keepdims=True)
    i1 = jnp.min(jnp.where(logits == m1, eidx, float(N_EXPERTS)), axis=0, keepdims=True)
    rest = jnp.where(eidx == i1, -jnp.inf, logits)
    m2 = jnp.max(rest, axis=0, keepdims=True)
    i2 = jnp.min(jnp.where(rest == m2, eidx, float(N_EXPERTS)), axis=0, keepdims=True)
    e2 = jnp.exp(m2 - m1)
    den = 1.0 + e2
    ri_ref[...] = jnp.concatenate([i1, i2], axis=0).astype(I32)
    rg_ref[...] = jnp.concatenate([1.0 / den, e2 / den], axis=0)


def _odd_mixer(h, gn, w1, b1, wdw, bdw, lg, lb, w2, b2, gf, wr, *, tile):
    b, s, d = h.shape
    n = b * s
    ncb = d // LANES
    nst = s // tile
    const2 = lambda *_: (0, 0)
    const3 = lambda *_: (0, 0, 0)
    wdw_b = jnp.transpose(wdw.reshape(CONV_K, ncb, LANES), (1, 0, 2))
    return pl.pallas_call(
        functools.partial(_odd_mixer_kernel, tile=tile),
        grid=(b, nst),
        in_specs=[
            pl.BlockSpec((1, tile, d), lambda i, j: (i, j, 0)),
            pl.BlockSpec((1, d), const2),
            pl.BlockSpec((d, 2 * d), const2),
            pl.BlockSpec((1, 2 * d), const2),
            pl.BlockSpec((ncb, CONV_K, LANES), const3),
            pl.BlockSpec((1, d), const2),
            pl.BlockSpec((1, d), const2),
            pl.BlockSpec((1, d), const2),
            pl.BlockSpec((d, d), const2),
            pl.BlockSpec((1, d), const2),
            pl.BlockSpec((1, d), const2),
            pl.BlockSpec((N_EXPERTS, d), const2),
        ],
        out_specs=[
            pl.BlockSpec((1, tile, d), lambda i, j: (i, j, 0)),
            pl.BlockSpec((tile, ncb, LANES), lambda i, j: (i * nst + j, 0, 0)),
            pl.BlockSpec((2, tile), lambda i, j: (0, i * nst + j)),
            pl.BlockSpec((2, tile), lambda i, j: (0, i * nst + j)),
        ],
        out_shape=[
            jax.ShapeDtypeStruct(h.shape, F32),
            jax.ShapeDtypeStruct((n, ncb, LANES), F32),
            jax.ShapeDtypeStruct((2, n), I32),
            jax.ShapeDtypeStruct((2, n), F32),
        ],
        scratch_shapes=[pltpu.VMEM((ncb, tile + CONV_HALO, LANES), F32), pltpu.VMEM((ncb, tile, LANES), F32)],
        compiler_params=_cparams(("arbitrary", "arbitrary")),
        name="odd_mixer",
    )(h, gn.reshape(1, d), w1.astype(BF16), b1.reshape(1, 2 * d), wdw_b, bdw.reshape(1, d),
      lg.reshape(1, d), lb.reshape(1, d), w2.astype(BF16), b2.reshape(1, d), gf.reshape(1, d), wr.T)


def _route_meta_kernel(ri_ref, pos_ref, tab_ref, *, tm):
    ri = ri_ref[...]
    i0, i1 = ri[0], ri[1]
    nb = i0.shape[0]
    shift = tm.bit_length() - 1
    r = lax.broadcasted_iota(I32, (LANES, LANES), 0)
    c = lax.broadcasted_iota(I32, (LANES, LANES), 1)
    upper = (r < c).astype(BF16)
    rb = lax.broadcasted_iota(I32, (nb, nb), 0)
    cb = lax.broadcasted_iota(I32, (nb, nb), 1)
    lower = (cb < rb).astype(BF16)
    lane = lax.broadcasted_iota(I32, (1, LANES), 1)
    pos0 = jnp.zeros(i0.shape, I32)
    pos1 = jnp.zeros(i0.shape, I32)
    goff = jnp.zeros((1, 1), I32)
    tile_expert = jnp.zeros((1, LANES), I32)
    pad_lo = jnp.zeros((1, LANES), I32)
    pad_hi = jnp.zeros((1, LANES), I32)
    for e in range(N_EXPERTS):
        m = jnp.logical_or(i0 == e, i1 == e).astype(F32)
        within = _dot(m.astype(BF16), upper)
        tot = jnp.broadcast_to(jnp.sum(m, axis=1, keepdims=True), (nb, LANES))
        blk = _dot(lower, tot.astype(BF16))
        cnt = jnp.sum(tot[:, 0:1], axis=0, keepdims=True).astype(I32)
        dest = goff + (blk + within).astype(I32)
        pos0 = jnp.where(i0 == e, dest, pos0)
        pos1 = jnp.where(i1 == e, dest, pos1)
        pad_lo = jnp.where(lane == e, goff + cnt, pad_lo)
        goff = goff + (((cnt + (tm - 1)) >> shift) << shift)
        pad_hi = jnp.where(lane == e, goff, pad_hi)
        tile_expert = tile_expert + ((lane << shift) >= goff).astype(I32)
    pos_ref[0] = pos0
    pos_ref[1] = pos1
    tab_ref[0:1, :] = jnp.minimum(tile_expert, N_EXPERTS - 1)
    tab_ref[1:2, :] = jnp.broadcast_to(goff >> shift, (1, LANES))
    tab_ref[2:3, :] = pad_lo
    tab_ref[3:4, :] = pad_hi
    tab_ref[4:8, :] = jnp.zeros((4, LANES), I32)


def _route_meta(ri, *, tm):
    n = ri.shape[1]
    nb = n // LANES
    pos, tab = pl.pallas_call(
        functools.partial(_route_meta_kernel, tm=tm),
        out_shape=[jax.ShapeDtypeStruct((2, nb, LANES), I32), jax.ShapeDtypeStruct((8, LANES), I32)],
        compiler_params=pltpu.CompilerParams(vmem_limit_bytes=VMEM_LIMIT),
        name="route_meta",
    )(ri.reshape(2, nb, LANES))
    return pos.reshape(2 * n), tab


def _route_inv_kernel(pos_ref, lo_ref, hi_ref, code_ref, *, n, tm):
    def pad(p, carry):
        code_ref[p] = 2 * n + (p & (2 * tm - 1))
        return carry

    for e in range(N_EXPERTS):
        lax.fori_loop(lo_ref[e], hi_ref[e], pad, 0)
    lax.fori_loop(hi_ref[N_EXPERTS - 1], code_ref.shape[0], pad, 0)

    def body(t, carry):
        code_ref[pos_ref[t]] = 2 * t
        code_ref[pos_ref[n + t]] = 2 * t + 1
        return carry

    lax.fori_loop(0, n, body, 0, unroll=8)


def _route_inv(pos, pad_lo, pad_hi, *, n, rows, tm):
    smem = pl.BlockSpec(memory_space=pltpu.SMEM)
    return pl.pallas_call(
        functools.partial(_route_inv_kernel, n=n, tm=tm),
        in_specs=[smem, smem, smem],
        out_specs=smem,
        out_shape=jax.ShapeDtypeStruct((rows,), I32),
        name="route_inv",
    )(pos, pad_lo, pad_hi)


def _moe_ffn_kernel(te_ref, na_ref, code_ref, hn_hbm, wg_hbm, wu_hbm, wd_hbm, o_hbm,
                    xbuf, ybuf, xb_ref, acc_ref, wg_ring, wu_ring, wd_ring, gsem, ssem, wsem,
                    *, layer, n, tm, cw):
    i = pl.program_id(0)
    na = na_ref[0]
    slot = i % 2
    other = 1 - slot
    d = xb_ref.shape[1]
    nc = wg_hbm.shape[3] // cw
    assert nc % 2 == 0, "ring slot of a tile's first chunk must not depend on the tile"

    def w_copies(e, c, ws):
        cs = pl.ds(c * cw, cw)
        return (pltpu.make_async_copy(wg_hbm.at[layer, e, :, cs], wg_ring.at[ws], wsem.at[ws]),
                pltpu.make_async_copy(wu_hbm.at[layer, e, :, cs], wu_ring.at[ws], wsem.at[ws]),
                pltpu.make_async_copy(wd_hbm.at[layer, e, cs, :], wd_ring.at[ws], wsem.at[ws]))

    def start_w(e, c, ws):
        for cp in w_copies(e, c, ws):
            cp.start()

    def wait_w(ws):
        for cp in w_copies(0, 0, ws):
            cp.wait()

    def gather_row(tile, sl, r):
        code = code_ref[tile * tm + r]
        tok = jnp.where(code < 2 * n, code >> 1, 0)
        pltpu.make_async_copy(hn_hbm.at[tok], xbuf.at[sl, r], gsem.at[sl]).start()

    def scatter_row(tile, sl, r, live):
        code = code_ref[tile * tm + r]
        dst = jnp.where(live, code, 2 * n + 2 * tm + r)
        pltpu.make_async_copy(ybuf.at[sl, r], o_hbm.at[dst], ssem.at[sl]).start()

    def wait_gather(sl):
        pltpu.make_async_copy(hn_hbm.at[pl.ds(0, tm)], xbuf.at[sl], gsem.at[sl]).wait()

    def wait_scatter(sl):
        pltpu.make_async_copy(ybuf.at[sl], o_hbm.at[pl.ds(0, tm)], ssem.at[sl]).wait()

    @pl.when(i == 0)
    def _():
        ybuf[...] = jnp.zeros(ybuf.shape, F32)
        for q in range(SPILL_TILES):
            cp = pltpu.make_async_copy(ybuf.at[0], o_hbm.at[pl.ds(2 * n + q * tm, tm)], ssem.at[0])
            cp.start()
            cp.wait()
        start_w(te_ref[0], 0, 0)

        def body(r, carry):
            gather_row(0, 0, r)
            return carry

        lax.fori_loop(0, tm, body, 0, unroll=8)

    @pl.when(i < na)
    def _():
        wait_gather(slot)
        xb_ref[...] = xbuf[slot].reshape(tm, d).astype(BF16)
        e = te_ref[i]
        nxt = jnp.minimum(i + 1, na - 1)
        e_nxt = te_ref[nxt]
        prev = jnp.maximum(i - 1, 0)
        live = i > 0
        y = None
        for c in range(nc):
            ws = c % 2
            wait_w(ws)
            if c + 1 < nc:
                start_w(e, c + 1, 1 - ws)
            else:
                start_w(e_nxt, 0, 1 - ws)
            x = xb_ref[...]
            a = (_silu(_dot(x, wg_ring[ws].astype(BF16))) * _dot(x, wu_ring[ws].astype(BF16))).astype(BF16)
            part = _dot(a, wd_ring[ws].astype(BF16))
            if c == 0:
                acc_ref[...] = part
            elif c + 1 < nc:
                acc_ref[...] += part
            else:
                y = acc_ref[...] + part
            for r in range(c * tm // nc, (c + 1) * tm // nc):
                gather_row(nxt, other, r)
                scatter_row(prev, other, r, live)

        @pl.when(i > 0)
        def _():
            wait_scatter(slot)

        ybuf[slot] = y.reshape(tm, d // LANES, LANES)

    @pl.when(i == na)
    def _():
        last = (na + 1) % 2

        def body(r, carry):
            scatter_row(na - 1, last, r, True)
            return carry

        lax.fori_loop(0, tm, body, 0, unroll=8)
        wait_scatter(last)
        wait_scatter(1 - last)
        wait_gather(1 - last)
        wait_w(0)


def _moe_ffn(hn, tile_expert, n_active, code, wg, wu, wd, *, layer, tm, cw):
    n, ncb, _ = hn.shape
    d = ncb * LANES
    rows = code.shape[0]
    hbm = pl.BlockSpec(memory_space=pl.ANY)
    return pl.pallas_call(
        functools.partial(_moe_ffn_kernel, layer=layer, n=n, tm=tm, cw=cw),
        grid_spec=pltpu.PrefetchScalarGridSpec(
            num_scalar_prefetch=3,
            grid=(rows // tm + 1,),
            in_specs=[hbm, hbm, hbm, hbm],
            out_specs=hbm,
            scratch_shapes=[
                pltpu.VMEM((2, tm, ncb, LANES), F32),
                pltpu.VMEM((2, tm, ncb, LANES), F32),
                pltpu.VMEM((tm, d), BF16),
                pltpu.VMEM((tm, d), F32),
                pltpu.VMEM((2, d, cw), F32),
                pltpu.VMEM((2, d, cw), F32),
                pltpu.VMEM((2, cw, d), F32),
                pltpu.SemaphoreType.DMA((2,)),
                pltpu.SemaphoreType.DMA((2,)),
                pltpu.SemaphoreType.DMA((2,)),
            ],
        ),
        out_shape=jax.ShapeDtypeStruct((2 * n + SPILL_TILES * tm, ncb, LANES), F32),
        compiler_params=_cparams(("arbitrary",)),
        name="moe_ffn",
    )(tile_expert, n_active, code, hn, wg, wu, wd)


def _moe_final_kernel(h_ref, g_ref, gf_ref, y_ref, o_ref):
    o_ref[...] = _rms(_moe_combine(h_ref[...], g_ref, y_ref), gf_ref[...])


def _moe_final(h, gates_t, y, gf, *, td):
    n, d = h.shape
    ncb = d // LANES
    return pl.pallas_call(
        _moe_final_kernel,
        grid=(n // td,),
        in_specs=[
            pl.BlockSpec((td, d), lambda i: (i, 0)),
            pl.BlockSpec((td, 2), lambda i: (i, 0)),
            pl.BlockSpec((1, d), lambda i: (0, 0)),
            pl.BlockSpec((td, 2, ncb, LANES), lambda i: (i, 0, 0, 0)),
        ],
        out_specs=pl.BlockSpec((td, d), lambda i: (i, 0)),
        out_shape=jax.ShapeDtypeStruct((n, d), F32),
        compiler_params=_cparams(("arbitrary",)),
        name="moe_add",
    )(h, gates_t, gf.reshape(1, d), y.reshape(y.shape[0] // 2, 2, ncb, LANES))


MIX_TILE = 512
FFN_TM = 512
FFN_SUB = 256
MOE_TM = 1024
MOE_CW = 256
ADD_TD = 512


def _moe_rows(n, tm):
    return ((2 * n + N_EXPERTS * (tm - 1)) // tm) * tm


def kernel(x, norm_mix_e, w_in_e, sgu_norm, w_spatial, b_spatial, w_pool, pool_scale, w_out_e, norm_ffn_e, w_gate_d, w_up_d, w_down_d, norm_mix_o, w_pw1, b_pw1, w_dw, b_dw, ln_g, ln_b, w_pw2, b_pw2, norm_ffn_o, w_router, w_gate_x, w_up_x, w_down_x, norm_final):
    b, s, d = x.shape
    n = b * s
    depth = norm_mix_e.shape[0] + norm_mix_o.shape[0]
    if depth % 2 == 1:
        raise NotImplementedError("trunk depth must end on an odd (MoE) layer")
    rows = _moe_rows(n, MOE_TM)
    h = x
    moe = None
    for layer in range(depth):
        i = layer // 2
        if layer % 2 == 0:
            h = _even_mixer(h, norm_mix_e[i], w_in_e[i], sgu_norm[i], w_spatial[i], b_spatial[i], w_pool[i],
                            pool_scale[i], w_out_e[i], tile=MIX_TILE, moe=moe)
            moe = None
            h = _dense_ffn(h.reshape(n, d), norm_ffn_e[i], w_gate_d[i], w_up_d[i], w_down_d[i],
                           tm=FFN_TM, sub=FFN_SUB).reshape(b, s, d)
        else:
            h, hn, ri, rg = _odd_mixer(h, norm_mix_o[i], w_pw1[i], b_pw1[i], w_dw[i], b_dw[i], ln_g[i], ln_b[i],
                                       w_pw2[i], b_pw2[i], norm_ffn_o[i], w_router[i], tile=MIX_TILE)
            pos, tab = _route_meta(ri, tm=MOE_TM)
            code = _route_inv(pos, tab[2, :N_EXPERTS], tab[3, :N_EXPERTS], n=n, rows=rows, tm=MOE_TM)
            ys = _moe_ffn(hn, tab[0, :rows // MOE_TM], tab[1, :1], code, w_gate_x, w_up_x, w_down_x,
                          layer=i, tm=MOE_TM, cw=MOE_CW)
            moe = (rg.T, ys)
    return _moe_final(h.reshape(n, d), *moe, norm_final, td=ADD_TD).reshape(b, s, d)
```

```python
import functools

import jax
import jax.numpy as jnp
from jax import lax
from jax.experimental import pallas as pl
from jax.experimental.pallas import tpu as pltpu

F32 = jnp.float32
BF16 = jnp.bfloat16
I32 = jnp.int32

EPS = 1e-6
CHUNK = 128
A_HEADS = 8
POOL_WINDOWS = (2, 4, 8, 16)
POOL_HALO = 16
CONV_K = 31
CONV_HALO = 32
N_EXPERTS = 8
SPILL_TILES = 3
LANES = 128
VMEM_LIMIT = 60 * 1024 * 1024


def _cparams(sem):
    return pltpu.CompilerParams(dimension_semantics=sem, vmem_limit_bytes=VMEM_LIMIT)


def _rms(x, g):
    return x * lax.rsqrt(jnp.mean(x * x, axis=-1, keepdims=True) + EPS) * g


def _gelu_tanh(x):
    c = 0.7978845608028654
    return 0.5 * x * (1.0 + jnp.tanh(c * (x + 0.044715 * (x * x * x))))


def _silu(x):
    return x * jax.nn.sigmoid(x)


def _dot(a, b):
    return jnp.dot(a, b, preferred_element_type=F32)


def _moe_combine(h, g_ref, y_ref):
    g = g_ref[...]
    return h + g[:, 0:1] * y_ref[:, 0].reshape(h.shape) + g[:, 1:2] * y_ref[:, 1].reshape(h.shape)


def _even_mixer_kernel(*refs, tile, combine):
    if combine:
        h_ref, g_ref, y_ref = refs[:3]
        refs = refs[3:]
    else:
        h_ref = refs[0]
        refs = refs[1:]
    gn_ref, win_ref, gv_ref, ws_ref, bst_ref, wp_ref, ps_ref, wout_ref, o_ref, zs_ref, cat_ref = refs
    s = pl.program_id(1)
    d = h_ref.shape[-1]
    aw = d // 2
    nc = tile // CHUNK
    h = h_ref[0]
    if combine:
        h = _moe_combine(h, g_ref, y_ref)
    hn = _rms(h, gn_ref[...]).astype(BF16)
    proj = _dot(hn, win_ref[...])
    u = _gelu_tanh(proj[:, :aw])
    v = _rms(_gelu_tanh(proj[:, aw:2 * aw]), gv_ref[...])
    z = proj[:, 2 * aw:]

    row = lax.broadcasted_iota(I32, (CHUNK, CHUNK), 0)
    col = lax.broadcasted_iota(I32, (CHUNK, CHUNK), 1)
    causal = row >= col
    lane = lax.broadcasted_iota(I32, (CHUNK, LANES), 1)
    lo = lane < (LANES // 2)
    for j in range(aw // LANES):
        w_a = jnp.where(causal, ws_ref[2 * j], 0).astype(BF16)
        w_b = jnp.where(causal, ws_ref[2 * j + 1], 0).astype(BF16)
        lhs = jnp.concatenate([w_a, w_b], axis=1)
        cols = []
        for c in range(nc):
            vb = v[c * CHUNK:(c + 1) * CHUNK, j * LANES:(j + 1) * LANES]
            cols.append(jnp.concatenate([jnp.where(lo, vb, 0.0), jnp.where(lo, 0.0, vb)], axis=0))
        rhs = jnp.concatenate(cols, axis=1).astype(BF16)
        mixed = _dot(lhs, rhs)
        bias = jnp.where(lo, bst_ref[:, 2 * j:2 * j + 1], bst_ref[:, 2 * j + 1:2 * j + 2])
        for c in range(nc):
            ub = u[c * CHUNK:(c + 1) * CHUNK, j * LANES:(j + 1) * LANES]
            a_out = ub * (mixed[:, c * LANES:(c + 1) * LANES] + bias)
            cat_ref[c * CHUNK:(c + 1) * CHUNK, j * LANES:(j + 1) * LANES] = a_out.astype(BF16)

    @pl.when(s == 0)
    def _():
        zs_ref[0:POOL_HALO, :] = jnp.zeros((POOL_HALO, aw), F32)

    @pl.when(s > 0)
    def _():
        zs_ref[0:POOL_HALO, :] = zs_ref[tile:tile + POOL_HALO, :]

    zs_ref[POOL_HALO:POOL_HALO + tile, :] = z
    pos = s * tile + 1 + lax.broadcasted_iota(I32, (tile, LANES), 0)
    for g, win in enumerate(POOL_WINDOWS):
        gs = slice(g * LANES, (g + 1) * LANES)
        acc = zs_ref[POOL_HALO:POOL_HALO + tile, gs]
        for i in range(1, win):
            acc = acc + zs_ref[POOL_HALO - i:POOL_HALO - i + tile, gs]
        cnt = jnp.minimum(pos, win).astype(F32)
        mixed = acc / cnt - z[:, gs]
        y = _dot(mixed.astype(BF16), wp_ref[g]) * ps_ref[:, gs]
        cat_ref[:, aw + g * LANES:aw + (g + 1) * LANES] = y.astype(BF16)

    o_ref[0] = h + _dot(cat_ref[...], wout_ref[...])


def _even_mixer(h, gn, w_in, gv, ws, bs, wp, ps, w_out, *, tile, moe=None):
    b, s, d = h.shape
    aw = d // 2
    nst = s // tile
    ncb = d // LANES
    const2 = lambda *_: (0, 0)
    const3 = lambda *_: (0, 0, 0)
    moe_specs, moe_args = [], []
    if moe is not None:
        gates_t, y = moe
        moe_specs = [pl.BlockSpec((tile, 2), lambda i, j: (i * nst + j, 0)),
                     pl.BlockSpec((tile, 2, ncb, LANES), lambda i, j: (i * nst + j, 0, 0, 0))]
        moe_args = [gates_t, y.reshape(y.shape[0] // 2, 2, ncb, LANES)]
    return pl.pallas_call(
        functools.partial(_even_mixer_kernel, tile=tile, combine=moe is not None),
        grid=(b, nst),
        in_specs=[
            pl.BlockSpec((1, tile, d), lambda i, j: (i, j, 0)),
            *moe_specs,
            pl.BlockSpec((1, d), const2),
            pl.BlockSpec((d, 3 * aw), const2),
            pl.BlockSpec((1, aw), const2),
            pl.BlockSpec((A_HEADS, CHUNK, CHUNK), const3),
            pl.BlockSpec((CHUNK, A_HEADS), const2),
            pl.BlockSpec((len(POOL_WINDOWS), LANES, LANES), const3),
            pl.BlockSpec((1, aw), const2),
            pl.BlockSpec((d, d), const2),
        ],
        out_specs=pl.BlockSpec((1, tile, d), lambda i, j: (i, j, 0)),
        out_shape=jax.ShapeDtypeStruct(h.shape, F32),
        scratch_shapes=[pltpu.VMEM((tile + POOL_HALO, aw), F32), pltpu.VMEM((tile, d), BF16)],
        compiler_params=_cparams(("arbitrary", "arbitrary")),
        name="even_mixer",
    )(h, *moe_args, gn.reshape(1, d), w_in.astype(BF16), gv.reshape(1, aw), ws, bs.T, wp.astype(BF16),
      ps.reshape(1, aw), w_out.astype(BF16))


def _swiglu_hidden(x, wg_ref, wu_ref, a_ref, *, sub):
    for c in range(a_ref.shape[1] // sub):
        cs = slice(c * sub, (c + 1) * sub)
        a_ref[:, cs] = (_silu(_dot(x, wg_ref[:, cs])) * _dot(x, wu_ref[:, cs])).astype(BF16)


def _dense_ffn_kernel(h_ref, gn_ref, wg_ref, wu_ref, wd_ref, o_ref, a_ref, *, sub):
    h = h_ref[...]
    x = _rms(h, gn_ref[...]).astype(BF16)
    _swiglu_hidden(x, wg_ref, wu_ref, a_ref, sub=sub)
    o_ref[...] = h + _dot(a_ref[...], wd_ref[...])


def _dense_ffn(h, gn, wg, wu, wd, *, tm, sub):
    n, d = h.shape
    f = wg.shape[1]
    const2 = lambda i: (0, 0)
    return pl.pallas_call(
        functools.partial(_dense_ffn_kernel, sub=sub),
        grid=(n // tm,),
        in_specs=[
            pl.BlockSpec((tm, d), lambda i: (i, 0)),
            pl.BlockSpec((1, d), const2),
            pl.BlockSpec((d, f), const2),
            pl.BlockSpec((d, f), const2),
            pl.BlockSpec((f, d), const2),
        ],
        out_specs=pl.BlockSpec((tm, d), lambda i: (i, 0)),
        out_shape=jax.ShapeDtypeStruct((n, d), F32),
        scratch_shapes=[pltpu.VMEM((tm, f), BF16)],
        compiler_params=_cparams(("arbitrary",)),
        name="dense_ffn",
    )(h, gn.reshape(1, d), wg.astype(BF16), wu.astype(BF16), wd.astype(BF16))


def _split3(x):
    hi = x.astype(BF16)
    lo = (x - hi.astype(F32)).astype(BF16)
    return hi, lo


def _odd_mixer_kernel(h_ref, gn_ref, w1_ref, b1_ref, wdw_ref, bdw_ref, lg_ref, lb_ref, w2_ref, b2_ref,
                      gf_ref, wr_ref, o_ref, hn_ref, ri_ref, rg_ref, xs_ref, ys_ref, *, tile):
    s = pl.program_id(1)
    d = h_ref.shape[-1]
    ncb = d // LANES
    h = h_ref[0]
    hn = _rms(h, gn_ref[...]).astype(BF16)
    a = _dot(hn, w1_ref[...]) + b1_ref[...]
    x = a[:, :d] * jax.nn.sigmoid(a[:, d:])

    @pl.when(s == 0)
    def _():
        xs_ref[:, 0:CONV_HALO, :] = jnp.zeros((ncb, CONV_HALO, LANES), F32)

    @pl.when(s > 0)
    def _():
        xs_ref[:, 0:CONV_HALO, :] = xs_ref[:, tile:tile + CONV_HALO, :]

    for cb in range(ncb):
        xs_ref[cb, CONV_HALO:CONV_HALO + tile, :] = x[:, cb * LANES:(cb + 1) * LANES]

    rb = 128
    base = CONV_HALO - (CONV_K - 1)

    def conv_block(cb, carry):
        for r0 in range(0, tile, rb):
            acc = jnp.zeros((rb, LANES), F32)
            for k in range(CONV_K):
                acc = acc + wdw_ref[cb, k:k + 1, :] * xs_ref[cb, r0 + base + k:r0 + base + k + rb, :]
            ys_ref[cb, r0:r0 + rb, :] = acc
        return carry

    lax.fori_loop(0, ncb, conv_block, 0)
    y = jnp.concatenate([ys_ref[cb] for cb in range(ncb)], axis=1) + bdw_ref[...]

    mu = jnp.mean(y, axis=-1, keepdims=True)
    yc = y - mu
    var = jnp.mean(yc * yc, axis=-1, keepdims=True)
    yn = _silu(yc * lax.rsqrt(var + EPS) * lg_ref[...] + lb_ref[...])
    h2 = h + _dot(yn.astype(BF16), w2_ref[...]) + b2_ref[...]
    o_ref[0] = h2

    hf = _rms(h2, gf_ref[...])
    hn_ref[...] = hf.reshape(tile, d // LANES, LANES)
    x_hi, x_lo = _split3(hf)
    w_hi, w_lo = _split3(wr_ref[...])
    nt = (((1,), (1,)), ((), ()))
    logits = (lax.dot_general(w_hi, x_hi, nt, preferred_element_type=F32)
              + lax.dot_general(w_lo, x_hi, nt, preferred_element_type=F32)
              + lax.dot_general(w_hi, x_lo, nt, preferred_element_type=F32))
    eidx = lax.broadcasted_iota(I32, logits.shape, 0).astype(F32)
    m1 = jnp.max(logits, axis=0, keepdims=True)
    i1 = jnp.min(jnp.where(logits == m1, eidx, float(N_EXPERTS)), axis=0, keepdims=True)
    rest = jnp.where(eidx == i1, -jnp.inf, logits)
    m2 = jnp.max(rest, axis=0, keepdims=True)
    i2 = jnp.min(jnp.where(rest == m2, eidx, float(N_EXPERTS)), axis=0, keepdims=True)
    e2 = jnp.exp(m2 - m1)
    den = 1.0 + e2
    ri_ref[...] = jnp.concatenate([i1, i2], axis=0).astype(I32)
    rg_ref[...] = jnp.concatenate([1.0 / den, e2 / den], axis=0)


def _odd_mixer(h, gn, w1, b1, wdw, bdw, lg, lb, w2, b2, gf, wr, *, tile):
    b, s, d = h.shape
    n = b * s
    ncb = d // LANES
    nst = s // tile
    const2 = lambda *_: (0, 0)
    const3 = lambda *_: (0, 0, 0)
    wdw_b = jnp.transpose(wdw.reshape(CONV_K, ncb, LANES), (1, 0, 2))
    return pl.pallas_call(
        functools.partial(_odd_mixer_kernel, tile=tile),
        grid=(b, nst),
        in_specs=[
            pl.BlockSpec((1, tile, d), lambda i, j: (i, j, 0)),
            pl.BlockSpec((1, d), const2),
            pl.BlockSpec((d, 2 * d), const2),
            pl.BlockSpec((1, 2 * d), const2),
            pl.BlockSpec((ncb, CONV_K, LANES), const3),
            pl.BlockSpec((1, d), const2),
            pl.BlockSpec((1, d), const2),
            pl.BlockSpec((1, d), const2),
            pl.BlockSpec((d, d), const2),
            pl.BlockSpec((1, d), const2),
            pl.BlockSpec((1, d), const2),
            pl.BlockSpec((N_EXPERTS, d), const2),
        ],
        out_specs=[
            pl.BlockSpec((1, tile, d), lambda i, j: (i, j, 0)),
            pl.BlockSpec((tile, ncb, LANES), lambda i, j: (i * nst + j, 0, 0)),
            pl.BlockSpec((2, tile), lambda i, j: (0, i * nst + j)),
            pl.BlockSpec((2, tile), lambda i, j: (0, i * nst + j)),
        ],
        out_shape=[
            jax.ShapeDtypeStruct(h.shape, F32),
            jax.ShapeDtypeStruct((n, ncb, LANES), F32),
            jax.ShapeDtypeStruct((2, n), I32),
            jax.ShapeDtypeStruct((2, n), F32),
        ],
        scratch_shapes=[pltpu.VMEM((ncb, tile + CONV_HALO, LANES), F32), pltpu.VMEM((ncb, tile, LANES), F32)],
        compiler_params=_cparams(("arbitrary", "arbitrary")),
        name="odd_mixer",
    )(h, gn.reshape(1, d), w1.astype(BF16), b1.reshape(1, 2 * d), wdw_b, bdw.reshape(1, d),
      lg.reshape(1, d), lb.reshape(1, d), w2.astype(BF16), b2.reshape(1, d), gf.reshape(1, d), wr.T)


def _route_meta_kernel(ri_ref, pos_ref, tab_ref, *, tm):
    ri = ri_ref[...]
    i0, i1 = ri[0], ri[1]
    nb = i0.shape[0]
    shift = tm.bit_length() - 1
    r = lax.broadcasted_iota(I32, (LANES, LANES), 0)
    c = lax.broadcasted_iota(I32, (LANES, LANES), 1)
    upper = (r < c).astype(BF16)
    rb = lax.broadcasted_iota(I32, (nb, nb), 0)
    cb = lax.broadcasted_iota(I32, (nb, nb), 1)
    lower = (cb < rb).astype(BF16)
    lane = lax.broadcasted_iota(I32, (1, LANES), 1)
    pos0 = jnp.zeros(i0.shape, I32)
    pos1 = jnp.zeros(i0.shape, I32)
    goff = jnp.zeros((1, 1), I32)
    tile_expert = jnp.zeros((1, LANES), I32)
    tile_rows = jnp.zeros((1, LANES), I32)
    pad_lo = jnp.zeros((1, LANES), I32)
    pad_hi = jnp.zeros((1, LANES), I32)
    for e in range(N_EXPERTS):
        m = jnp.logical_or(i0 == e, i1 == e).astype(F32)
        within = _dot(m.astype(BF16), upper)
        tot = jnp.broadcast_to(jnp.sum(m, axis=1, keepdims=True), (nb, LANES))
        blk = _dot(lower, tot.astype(BF16))
        cnt = jnp.sum(tot[:, 0:1], axis=0, keepdims=True).astype(I32)
        dest = goff + (blk + within).astype(I32)
        pos0 = jnp.where(i0 == e, dest, pos0)
        pos1 = jnp.where(i1 == e, dest, pos1)
        pad_lo = jnp.where(lane == e, goff + cnt, pad_lo)
        tile_start = lane << shift
        in_group = tile_start >= goff
        tile_rows = jnp.where(in_group, jnp.clip(goff + cnt - tile_start, 0, tm), tile_rows)
        goff = goff + (((cnt + (tm - 1)) >> shift) << shift)
        pad_hi = jnp.where(lane == e, goff, pad_hi)
        tile_expert = tile_expert + ((lane << shift) >= goff).astype(I32)
    pos_ref[0] = pos0
    pos_ref[1] = pos1
    tab_ref[0:1, :] = jnp.minimum(tile_expert, N_EXPERTS - 1)
    tab_ref[1:2, :] = jnp.broadcast_to(goff >> shift, (1, LANES))
    tab_ref[2:3, :] = pad_lo
    tab_ref[3:4, :] = pad_hi
    tab_ref[4:5, :] = tile_rows
    tab_ref[5:8, :] = jnp.zeros((3, LANES), I32)


def _route_meta(ri, *, tm):
    n = ri.shape[1]
    nb = n // LANES
    pos, tab = pl.pallas_call(
        functools.partial(_route_meta_kernel, tm=tm),
        out_shape=[jax.ShapeDtypeStruct((2, nb, LANES), I32), jax.ShapeDtypeStruct((8, LANES), I32)],
        compiler_params=pltpu.CompilerParams(vmem_limit_bytes=VMEM_LIMIT),
        name="route_meta",
    )(ri.reshape(2, nb, LANES))
    return pos.reshape(2 * n), tab


def _route_inv_kernel(pos_ref, lo_ref, hi_ref, code_ref, *, n, tm):
    def pad(p, carry):
        code_ref[p] = 2 * n + (p & (2 * tm - 1))
        return carry

    for e in range(N_EXPERTS):
        lax.fori_loop(lo_ref[e], hi_ref[e], pad, 0)
    lax.fori_loop(hi_ref[N_EXPERTS - 1], code_ref.shape[0], pad, 0)

    def body(t, carry):
        code_ref[pos_ref[t]] = 2 * t
        code_ref[pos_ref[n + t]] = 2 * t + 1
        return carry

    lax.fori_loop(0, n, body, 0, unroll=8)


def _route_inv(pos, pad_lo, pad_hi, *, n, rows, tm):
    smem = pl.BlockSpec(memory_space=pltpu.SMEM)
    return pl.pallas_call(
        functools.partial(_route_inv_kernel, n=n, tm=tm),
        in_specs=[smem, smem, smem],
        out_specs=smem,
        out_shape=jax.ShapeDtypeStruct((rows,), I32),
        name="route_inv",
    )(pos, pad_lo, pad_hi)


def _moe_ffn_kernel(te_ref, nr_ref, na_ref, code_ref, hn_hbm, wg_hbm, wu_hbm, wd_hbm, o_hbm,
                    xbuf, ybuf, xb_ref, acc_ref, wg_ring, wu_ring, wd_ring, gsem, ssem, wsem,
                    *, layer, n, tm, cw, roll, m_steps):
    i = pl.program_id(0)
    na = na_ref[0]
    xs = i % 2
    xo = 1 - xs
    nb = ybuf.shape[0]
    ys = i % nb
    yp = (i + nb - 1) % nb
    d = xb_ref.shape[1]
    nc = wg_hbm.shape[3] // cw
    rd = wg_ring.shape[0]
    ahead = rd - 1
    assert ahead < nc and nc % roll == 0 and nb in (1, 2)

    def ring_slot(tile, c):
        return (tile * nc + c) % rd

    def w_copies(e, c, ws):
        cs = pl.ds(pl.multiple_of(c * cw, cw), cw)
        return (pltpu.make_async_copy(wg_hbm.at[layer, e, :, cs], wg_ring.at[ws], wsem.at[ws]),
                pltpu.make_async_copy(wu_hbm.at[layer, e, :, cs], wu_ring.at[ws], wsem.at[ws]),
                pltpu.make_async_copy(wd_hbm.at[layer, e, cs, :], wd_ring.at[ws], wsem.at[ws]))

    def start_w(e, c, ws):
        for cp in w_copies(e, c, ws):
            cp.start()

    def wait_w(ws):
        for cp in w_copies(0, 0, ws):
            cp.wait()

    def gather_row(tile, sl, r):
        code = code_ref[tile * tm + r]
        tok = jnp.where(code < 2 * n, code >> 1, 0)
        pltpu.make_async_copy(hn_hbm.at[tok], xbuf.at[sl, r], gsem.at[sl]).start()

    def scatter_row(tile, sl, r, live):
        code = code_ref[tile * tm + r]
        dst = jnp.where(live, code, 2 * n + 2 * tm + r)
        pltpu.make_async_copy(ybuf.at[sl, r], o_hbm.at[dst], ssem.at[sl]).start()

    def wait_gather(sl):
        pltpu.make_async_copy(hn_hbm.at[pl.ds(0, tm)], xbuf.at[sl], gsem.at[sl]).wait()

    def wait_scatter(sl):
        pltpu.make_async_copy(ybuf.at[sl], o_hbm.at[pl.ds(0, tm)], ssem.at[sl]).wait()

    @pl.when(i == 0)
    def _():
        ybuf[...] = jnp.zeros(ybuf.shape, F32)
        for q in range(SPILL_TILES):
            cp = pltpu.make_async_copy(ybuf.at[0], o_hbm.at[pl.ds(2 * n + q * tm, tm)], ssem.at[0])
            cp.start()
            cp.wait()
        for c in range(ahead):
            start_w(te_ref[0], c, ring_slot(0, c))

        def body(r, carry):
            gather_row(0, 0, r)
            return carry

        lax.fori_loop(0, tm, body, 0, unroll=8)

    def tile_body(m):
        xb_ref[0:m] = xbuf[xs, 0:m].reshape(m, d).astype(BF16)
        e = te_ref[i]
        nxt = jnp.minimum(i + 1, na - 1)
        e_nxt = te_ref[nxt]
        prev = jnp.maximum(i - 1, 0)
        live = i > 0
        acc_ref[0:m] = jnp.zeros((m, d), F32)
        rpc = tm // nc

        def chunks(it, carry):
            for u in range(roll):
                c = it * roll + u
                ws = ring_slot(i, c)
                wait_w(ws)
                ca = c + ahead
                wrap = ca >= nc
                start_w(jnp.where(wrap, e_nxt, e), jnp.where(wrap, ca - nc, ca), ring_slot(i, ca))
                x = xb_ref[0:m]
                a = (_silu(_dot(x, wg_ring[ws].astype(BF16))) * _dot(x, wu_ring[ws].astype(BF16))).astype(BF16)
                acc_ref[0:m] += _dot(a, wd_ring[ws].astype(BF16))
                for q in range(rpc):
                    gather_row(nxt, xo, c * rpc + q)
                    scatter_row(prev, yp, c * rpc + q, live)
            return carry

        lax.fori_loop(0, nc // roll, chunks, 0)
        for r in range(nc * rpc, tm):
            gather_row(nxt, xo, r)
            scatter_row(prev, yp, r, live)

        @pl.when(i >= nb - 1)
        def _():
            wait_scatter(ys)

        ybuf[ys, 0:m] = acc_ref[0:m].reshape(m, d // LANES, LANES)

    @pl.when(i < na)
    def _():
        wait_gather(xs)
        real = nr_ref[i]
        lo = 0
        for m in m_steps:
            @pl.when(jnp.logical_and(real > lo, real <= m))
            def _(m=m):
                tile_body(m)

            lo = m

    @pl.when(i == na)
    def _():
        def body(r, carry):
            scatter_row(na - 1, yp, r, True)
            return carry

        lax.fori_loop(0, tm, body, 0, unroll=8)
        wait_scatter(yp)
        if nb == 2:
            wait_scatter(1 - yp)
        wait_gather(xs)
        for c in range(ahead):
            wait_w(ring_slot(na, c))


def _moe_ffn(hn, tile_expert, tile_rows, n_active, code, wg, wu, wd, *, layer, tm, cw, ring, roll, ybufs, m_steps):
    n, ncb, _ = hn.shape
    d = ncb * LANES
    rows = code.shape[0]
    hbm = pl.BlockSpec(memory_space=pl.ANY)
    return pl.pallas_call(
        functools.partial(_moe_ffn_kernel, layer=layer, n=n, tm=tm, cw=cw, roll=roll, m_steps=m_steps),
        grid_spec=pltpu.PrefetchScalarGridSpec(
            num_scalar_prefetch=4,
            grid=(rows // tm + 1,),
            in_specs=[hbm, hbm, hbm, hbm],
            out_specs=hbm,
            scratch_shapes=[
                pltpu.VMEM((2, tm, ncb, LANES), F32),
                pltpu.VMEM((ybufs, tm, ncb, LANES), F32),
                pltpu.VMEM((tm, d), BF16),
                pltpu.VMEM((tm, d), F32),
                pltpu.VMEM((ring, d, cw), F32),
                pltpu.VMEM((ring, d, cw), F32),
                pltpu.VMEM((ring, cw, d), F32),
                pltpu.SemaphoreType.DMA((2,)),
                pltpu.SemaphoreType.DMA((2,)),
                pltpu.SemaphoreType.DMA((ring,)),
            ],
        ),
        out_shape=jax.ShapeDtypeStruct((2 * n + SPILL_TILES * tm, ncb, LANES), F32),
        compiler_params=_cparams(("arbitrary",)),
        name="moe_ffn",
    )(tile_expert, tile_rows, n_active, code, hn, wg, wu, wd)


def _moe_final_kernel(h_ref, g_ref, gf_ref, y_ref, o_ref):
    o_ref[...] = _rms(_moe_combine(h_ref[...], g_ref, y_ref), gf_ref[...])


def _moe_final(h, gates_t, y, gf, *, td):
    n, d = h.shape
    ncb = d // LANES
    return pl.pallas_call(
        _moe_final_kernel,
        grid=(n // td,),
        in_specs=[
            pl.BlockSpec((td, d), lambda i: (i, 0)),
            pl.BlockSpec((td, 2), lambda i: (i, 0)),
            pl.BlockSpec((1, d), lambda i: (0, 0)),
            pl.BlockSpec((td, 2, ncb, LANES), lambda i: (i, 0, 0, 0)),
        ],
        out_specs=pl.BlockSpec((td, d), lambda i: (i, 0)),
        out_shape=jax.ShapeDtypeStruct((n, d), F32),
        compiler_params=_cparams(("arbitrary",)),
        name="moe_add",
    )(h, gates_t, gf.reshape(1, d), y.reshape(y.shape[0] // 2, 2, ncb, LANES))


MIX_TILE = 512
FFN_TM = 512
FFN_SUB = 256
MOE_TM = 1024
MOE_CW = 256
MOE_RING = 4
MOE_ROLL = 2
MOE_YBUFS = 2
MOE_M_STEPS = (256, 512, 768, 1024)
ADD_TD = 512


def _moe_rows(n, tm):
    return ((2 * n + N_EXPERTS * (tm - 1)) // tm) * tm


def kernel(x, norm_mix_e, w_in_e, sgu_norm, w_spatial, b_spatial, w_pool, pool_scale, w_out_e, norm_ffn_e, w_gate_d, w_up_d, w_down_d, norm_mix_o, w_pw1, b_pw1, w_dw, b_dw, ln_g, ln_b, w_pw2, b_pw2, norm_ffn_o, w_router, w_gate_x, w_up_x, w_down_x, norm_final):
    b, s, d = x.shape
    n = b * s
    depth = norm_mix_e.shape[0] + norm_mix_o.shape[0]
    if depth % 2 == 1:
        raise NotImplementedError("trunk depth must end on an odd (MoE) layer")
    rows = _moe_rows(n, MOE_TM)
    h = x
    moe = None
    for layer in range(depth):
        i = layer // 2
        if layer % 2 == 0:
            h = _even_mixer(h, norm_mix_e[i], w_in_e[i], sgu_norm[i], w_spatial[i], b_spatial[i], w_pool[i],
                            pool_scale[i], w_out_e[i], tile=MIX_TILE, moe=moe)
            moe = None
            h = _dense_ffn(h.reshape(n, d), norm_ffn_e[i], w_gate_d[i], w_up_d[i], w_down_d[i],
                           tm=FFN_TM, sub=FFN_SUB).reshape(b, s, d)
        else:
            h, hn, ri, rg = _odd_mixer(h, norm_mix_o[i], w_pw1[i], b_pw1[i], w_dw[i], b_dw[i], ln_g[i], ln_b[i],
                                       w_pw2[i], b_pw2[i], norm_ffn_o[i], w_router[i], tile=MIX_TILE)
            pos, tab = _route_meta(ri, tm=MOE_TM)
            code = _route_inv(pos, tab[2, :N_EXPERTS], tab[3, :N_EXPERTS], n=n, rows=rows, tm=MOE_TM)
            nt = rows // MOE_TM
            ys = _moe_ffn(hn, tab[0, :nt], tab[4, :nt], tab[1, :1], code, w_gate_x, w_up_x, w_down_x,
                          layer=i, tm=MOE_TM, cw=MOE_CW, ring=MOE_RING, roll=MOE_ROLL, ybufs=MOE_YBUFS,
                          m_steps=MOE_M_STEPS)
            moe = (rg.T, ys)
    return _moe_final(h.reshape(n, d), *moe, norm_final, td=ADD_TD).reshape(b, s, d)
```

```python
import functools

import jax
import jax.numpy as jnp
from jax import lax
from jax.experimental import pallas as pl
from jax.experimental.pallas import tpu as pltpu

F32 = jnp.float32
BF16 = jnp.bfloat16
I32 = jnp.int32

EPS = 1e-6
CHUNK = 128
A_HEADS = 8
POOL_WINDOWS = (2, 4, 8, 16)
POOL_HALO = 16
CONV_K = 31
CONV_HALO = 32
N_EXPERTS = 8
SPILL_TILES = 3
LANES = 128
VMEM_LIMIT = 60 * 1024 * 1024


def _cparams(sem):
    return pltpu.CompilerParams(dimension_semantics=sem, vmem_limit_bytes=VMEM_LIMIT)


def _rms(x, g):
    return x * lax.rsqrt(jnp.mean(x * x, axis=-1, keepdims=True) + EPS) * g


def _gelu_tanh(x):
    c = 0.7978845608028654
    return 0.5 * x * (1.0 + jnp.tanh(c * (x + 0.044715 * (x * x * x))))


def _silu(x):
    return x * jax.nn.sigmoid(x)


def _dot(a, b):
    return jnp.dot(a, b, preferred_element_type=F32)


def _moe_combine(h, g_ref, y_ref):
    g = g_ref[...]
    return h + g[:, 0:1] * y_ref[:, 0].reshape(h.shape) + g[:, 1:2] * y_ref[:, 1].reshape(h.shape)


def _even_mixer_kernel(*refs, tile, sub, combine):
    if combine:
        h_ref, g_ref, y_ref = refs[:3]
        refs = refs[3:]
    else:
        h_ref = refs[0]
        refs = refs[1:]
    gn_ref, win_ref, gv_ref, ws_ref, bst_ref, wp_ref, ps_ref, wout_ref, o_ref, zs_ref, cat_ref = refs
    s = pl.program_id(1)
    d = h_ref.shape[-1]
    aw = d // 2
    nc = sub // CHUNK
    row = lax.broadcasted_iota(I32, (CHUNK, CHUNK), 0)
    col = lax.broadcasted_iota(I32, (CHUNK, CHUNK), 1)
    causal = row >= col
    lane = lax.broadcasted_iota(I32, (CHUNK, LANES), 1)
    lo = lane < (LANES // 2)

    @pl.when(s == 0)
    def _():
        zs_ref[0:POOL_HALO, :] = jnp.zeros((POOL_HALO, aw), F32)

    @pl.when(s > 0)
    def _():
        zs_ref[0:POOL_HALO, :] = zs_ref[tile:tile + POOL_HALO, :]

    for r0 in range(0, tile, sub):
        rows = slice(r0, r0 + sub)
        h = h_ref[0, rows, :]
        if combine:
            h = _moe_combine(h, g_ref.at[rows], y_ref.at[rows])
        hn = _rms(h, gn_ref[...]).astype(BF16)
        proj = _dot(hn, win_ref[...])
        u = _gelu_tanh(proj[:, :aw])
        v = _rms(_gelu_tanh(proj[:, aw:2 * aw]), gv_ref[...])
        z = proj[:, 2 * aw:]

        for j in range(aw // LANES):
            w_a = jnp.where(causal, ws_ref[2 * j], 0).astype(BF16)
            w_b = jnp.where(causal, ws_ref[2 * j + 1], 0).astype(BF16)
            lhs = jnp.concatenate([w_a, w_b], axis=1)
            cols = []
            for c in range(nc):
                vb = v[c * CHUNK:(c + 1) * CHUNK, j * LANES:(j + 1) * LANES]
                cols.append(jnp.concatenate([jnp.where(lo, vb, 0.0), jnp.where(lo, 0.0, vb)], axis=0))
            rhs = jnp.concatenate(cols, axis=1).astype(BF16)
            mixed = _dot(lhs, rhs)
            bias = jnp.where(lo, bst_ref[:, 2 * j:2 * j + 1], bst_ref[:, 2 * j + 1:2 * j + 2])
            for c in range(nc):
                ub = u[c * CHUNK:(c + 1) * CHUNK, j * LANES:(j + 1) * LANES]
                a_out = ub * (mixed[:, c * LANES:(c + 1) * LANES] + bias)
                cat_ref[r0 + c * CHUNK:r0 + (c + 1) * CHUNK, j * LANES:(j + 1) * LANES] = a_out.astype(BF16)

        zs_ref[POOL_HALO + r0:POOL_HALO + r0 + sub, :] = z
        pos = s * tile + r0 + 1 + lax.broadcasted_iota(I32, (sub, LANES), 0)
        for g, win in enumerate(POOL_WINDOWS):
            gs = slice(g * LANES, (g + 1) * LANES)
            acc = zs_ref[POOL_HALO + r0:POOL_HALO + r0 + sub, gs]
            for i in range(1, win):
                acc = acc + zs_ref[POOL_HALO + r0 - i:POOL_HALO + r0 - i + sub, gs]
            cnt = jnp.minimum(pos, win).astype(F32)
            mixed = acc / cnt - z[:, gs]
            y = _dot(mixed.astype(BF16), wp_ref[g]) * ps_ref[:, gs]
            cat_ref[rows, aw + g * LANES:aw + (g + 1) * LANES] = y.astype(BF16)

        o_ref[0, rows, :] = h + _dot(cat_ref[rows, :], wout_ref[...])


def _even_mixer(h, gn, w_in, gv, ws, bs, wp, ps, w_out, *, tile, sub, moe=None):
    b, s, d = h.shape
    aw = d // 2
    nst = s // tile
    ncb = d // LANES
    const2 = lambda *_: (0, 0)
    const3 = lambda *_: (0, 0, 0)
    moe_specs, moe_args = [], []
    if moe is not None:
        gates_t, y = moe
        moe_specs = [pl.BlockSpec((tile, 2), lambda i, j: (i * nst + j, 0)),
                     pl.BlockSpec((tile, 2, ncb, LANES), lambda i, j: (i * nst + j, 0, 0, 0))]
        moe_args = [gates_t, y.reshape(y.shape[0] // 2, 2, ncb, LANES)]
    return pl.pallas_call(
        functools.partial(_even_mixer_kernel, tile=tile, sub=sub, combine=moe is not None),
        grid=(b, nst),
        in_specs=[
            pl.BlockSpec((1, tile, d), lambda i, j: (i, j, 0)),
            *moe_specs,
            pl.BlockSpec((1, d), const2),
            pl.BlockSpec((d, 3 * aw), const2),
            pl.BlockSpec((1, aw), const2),
            pl.BlockSpec((A_HEADS, CHUNK, CHUNK), const3),
            pl.BlockSpec((CHUNK, A_HEADS), const2),
            pl.BlockSpec((len(POOL_WINDOWS), LANES, LANES), const3),
            pl.BlockSpec((1, aw), const2),
            pl.BlockSpec((d, d), const2),
        ],
        out_specs=pl.BlockSpec((1, tile, d), lambda i, j: (i, j, 0)),
        out_shape=jax.ShapeDtypeStruct(h.shape, F32),
        scratch_shapes=[pltpu.VMEM((tile + POOL_HALO, aw), F32), pltpu.VMEM((tile, d), BF16)],
        compiler_params=_cparams(("arbitrary", "arbitrary")),
        name="even_mixer",
    )(h, *moe_args, gn.reshape(1, d), w_in.astype(BF16), gv.reshape(1, aw), ws, bs.T, wp.astype(BF16),
      ps.reshape(1, aw), w_out.astype(BF16))


def _swiglu_hidden(x, wg_ref, wu_ref, a_ref, *, sub):
    for c in range(a_ref.shape[1] // sub):
        cs = slice(c * sub, (c + 1) * sub)
        a_ref[:, cs] = (_silu(_dot(x, wg_ref[:, cs])) * _dot(x, wu_ref[:, cs])).astype(BF16)


def _dense_ffn_kernel(h_ref, gn_ref, wg_ref, wu_ref, wd_ref, o_ref, a_ref, *, sub):
    h = h_ref[...]
    x = _rms(h, gn_ref[...]).astype(BF16)
    _swiglu_hidden(x, wg_ref, wu_ref, a_ref, sub=sub)
    o_ref[...] = h + _dot(a_ref[...], wd_ref[...])


def _dense_ffn(h, gn, wg, wu, wd, *, tm, sub):
    n, d = h.shape
    f = wg.shape[1]
    const2 = lambda i: (0, 0)
    return pl.pallas_call(
        functools.partial(_dense_ffn_kernel, sub=sub),
        grid=(n // tm,),
        in_specs=[
            pl.BlockSpec((tm, d), lambda i: (i, 0)),
            pl.BlockSpec((1, d), const2),
            pl.BlockSpec((d, f), const2),
            pl.BlockSpec((d, f), const2),
            pl.BlockSpec((f, d), const2),
        ],
        out_specs=pl.BlockSpec((tm, d), lambda i: (i, 0)),
        out_shape=jax.ShapeDtypeStruct((n, d), F32),
        scratch_shapes=[pltpu.VMEM((tm, f), BF16)],
        compiler_params=_cparams(("arbitrary",)),
        name="dense_ffn",
    )(h, gn.reshape(1, d), wg.astype(BF16), wu.astype(BF16), wd.astype(BF16))


def _split3(x):
    hi = x.astype(BF16)
    lo = (x - hi.astype(F32)).astype(BF16)
    return hi, lo


def _odd_mixer_kernel(h_ref, gn_ref, w1_ref, b1_ref, wdw_ref, bdw_ref, lg_ref, lb_ref, w2_ref, b2_ref,
                      gf_ref, wr_ref, o_ref, hn_ref, ri_ref, rg_ref, xs_ref, ys_ref, *, tile):
    s = pl.program_id(1)
    d = h_ref.shape[-1]
    ncb = d // LANES
    h = h_ref[0]
    hn = _rms(h, gn_ref[...]).astype(BF16)
    a = _dot(hn, w1_ref[...]) + b1_ref[...]
    x = a[:, :d] * jax.nn.sigmoid(a[:, d:])

    @pl.when(s == 0)
    def _():
        xs_ref[:, 0:CONV_HALO, :] = jnp.zeros((ncb, CONV_HALO, LANES), F32)

    @pl.when(s > 0)
    def _():
        xs_ref[:, 0:CONV_HALO, :] = xs_ref[:, tile:tile + CONV_HALO, :]

    for cb in range(ncb):
        xs_ref[cb, CONV_HALO:CONV_HALO + tile, :] = x[:, cb * LANES:(cb + 1) * LANES]

    rb = 128
    base = CONV_HALO - (CONV_K - 1)

    def conv_block(cb, carry):
        for r0 in range(0, tile, rb):
            acc = jnp.zeros((rb, LANES), F32)
            for k in range(CONV_K):
                acc = acc + wdw_ref[cb, k:k + 1, :] * xs_ref[cb, r0 + base + k:r0 + base + k + rb, :]
            ys_ref[cb, r0:r0 + rb, :] = acc
        return carry

    lax.fori_loop(0, ncb, conv_block, 0)
    y = jnp.concatenate([ys_ref[cb] for cb in range(ncb)], axis=1) + bdw_ref[...]

    mu = jnp.mean(y, axis=-1, keepdims=True)
    yc = y - mu
    var = jnp.mean(yc * yc, axis=-1, keepdims=True)
    yn = _silu(yc * lax.rsqrt(var + EPS) * lg_ref[...] + lb_ref[...])
    h2 = h + _dot(yn.astype(BF16), w2_ref[...]) + b2_ref[...]
    o_ref[0] = h2

    hf = _rms(h2, gf_ref[...])
    hn_ref[...] = hf.reshape(tile, d // LANES, LANES)
    x_hi, x_lo = _split3(hf)
    w_hi, w_lo = _split3(wr_ref[...])
    nt = (((1,), (1,)), ((), ()))
    logits = (lax.dot_general(w_hi, x_hi, nt, preferred_element_type=F32)
              + lax.dot_general(w_lo, x_hi, nt, preferred_element_type=F32)
              + lax.dot_general(w_hi, x_lo, nt, preferred_element_type=F32))
    eidx = lax.broadcasted_iota(I32, logits.shape, 0).astype(F32)
    m1 = jnp.max(logits, axis=0, keepdims=True)
    i1 = jnp.min(jnp.where(logits == m1, eidx, float(N_EXPERTS)), axis=0, keepdims=True)
    rest = jnp.where(eidx == i1, -jnp.inf, logits)
    m2 = jnp.max(rest, axis=0, keepdims=True)
    i2 = jnp.min(jnp.where(rest == m2, eidx, float(N_EXPERTS)), axis=0, keepdims=True)
    e2 = jnp.exp(m2 - m1)
    den = 1.0 + e2
    ri_ref[...] = jnp.concatenate([i1, i2], axis=0).astype(I32)
    rg_ref[...] = jnp.concatenate([1.0 / den, e2 / den], axis=0)


def _odd_mixer(h, gn, w1, b1, wdw, bdw, lg, lb, w2, b2, gf, wr, *, tile):
    b, s, d = h.shape
    n = b * s
    ncb = d // LANES
    nst = s // tile
    const2 = lambda *_: (0, 0)
    const3 = lambda *_: (0, 0, 0)
    wdw_b = jnp.transpose(wdw.reshape(CONV_K, ncb, LANES), (1, 0, 2))
    return pl.pallas_call(
        functools.partial(_odd_mixer_kernel, tile=tile),
        grid=(b, nst),
        in_specs=[
            pl.BlockSpec((1, tile, d), lambda i, j: (i, j, 0)),
            pl.BlockSpec((1, d), const2),
            pl.BlockSpec((d, 2 * d), const2),
            pl.BlockSpec((1, 2 * d), const2),
            pl.BlockSpec((ncb, CONV_K, LANES), const3),
            pl.BlockSpec((1, d), const2),
            pl.BlockSpec((1, d), const2),
            pl.BlockSpec((1, d), const2),
            pl.BlockSpec((d, d), const2),
            pl.BlockSpec((1, d), const2),
            pl.BlockSpec((1, d), const2),
            pl.BlockSpec((N_EXPERTS, d), const2),
        ],
        out_specs=[
            pl.BlockSpec((1, tile, d), lambda i, j: (i, j, 0)),
            pl.BlockSpec((tile, ncb, LANES), lambda i, j: (i * nst + j, 0, 0)),
            pl.BlockSpec((2, tile), lambda i, j: (0, i * nst + j)),
            pl.BlockSpec((2, tile), lambda i, j: (0, i * nst + j)),
        ],
        out_shape=[
            jax.ShapeDtypeStruct(h.shape, F32),
            jax.ShapeDtypeStruct((n, ncb, LANES), F32),
            jax.ShapeDtypeStruct((2, n), I32),
            jax.ShapeDtypeStruct((2, n), F32),
        ],
        scratch_shapes=[pltpu.VMEM((ncb, tile + CONV_HALO, LANES), F32), pltpu.VMEM((ncb, tile, LANES), F32)],
        compiler_params=_cparams(("arbitrary", "arbitrary")),
        name="odd_mixer",
    )(h, gn.reshape(1, d), w1.astype(BF16), b1.reshape(1, 2 * d), wdw_b, bdw.reshape(1, d),
      lg.reshape(1, d), lb.reshape(1, d), w2.astype(BF16), b2.reshape(1, d), gf.reshape(1, d), wr.T)


def _route_meta_kernel(ri_ref, pos_ref, tab_ref, *, tm):
    ri = ri_ref[...]
    i0, i1 = ri[0], ri[1]
    nb = i0.shape[0]
    shift = tm.bit_length() - 1
    r = lax.broadcasted_iota(I32, (LANES, LANES), 0)
    c = lax.broadcasted_iota(I32, (LANES, LANES), 1)
    upper = (r < c).astype(BF16)
    rb = lax.broadcasted_iota(I32, (nb, nb), 0)
    cb = lax.broadcasted_iota(I32, (nb, nb), 1)
    lower = (cb < rb).astype(BF16)
    lane = lax.broadcasted_iota(I32, (1, LANES), 1)
    pos0 = jnp.zeros(i0.shape, I32)
    pos1 = jnp.zeros(i0.shape, I32)
    goff = jnp.zeros((1, 1), I32)
    tile_expert = jnp.zeros((1, LANES), I32)
    tile_rows = jnp.zeros((1, LANES), I32)
    pad_lo = jnp.zeros((1, LANES), I32)
    pad_hi = jnp.zeros((1, LANES), I32)
    for e in range(N_EXPERTS):
        m = jnp.logical_or(i0 == e, i1 == e).astype(F32)
        within = _dot(m.astype(BF16), upper)
        tot = jnp.broadcast_to(jnp.sum(m, axis=1, keepdims=True), (nb, LANES))
        blk = _dot(lower, tot.astype(BF16))
        cnt = jnp.sum(tot[:, 0:1], axis=0, keepdims=True).astype(I32)
        dest = goff + (blk + within).astype(I32)
        pos0 = jnp.where(i0 == e, dest, pos0)
        pos1 = jnp.where(i1 == e, dest, pos1)
        pad_lo = jnp.where(lane == e, goff + cnt, pad_lo)
        tile_start = lane << shift
        in_group = tile_start >= goff
        tile_rows = jnp.where(in_group, jnp.clip(goff + cnt - tile_start, 0, tm), tile_rows)
        goff = goff + (((cnt + (tm - 1)) >> shift) << shift)
        pad_hi = jnp.where(lane == e, goff, pad_hi)
        tile_expert = tile_expert + ((lane << shift) >= goff).astype(I32)
    pos_ref[0] = pos0
    pos_ref[1] = pos1
    tab_ref[0:1, :] = jnp.minimum(tile_expert, N_EXPERTS - 1)
    tab_ref[1:2, :] = jnp.broadcast_to(goff >> shift, (1, LANES))
    tab_ref[2:3, :] = pad_lo
    tab_ref[3:4, :] = pad_hi
    tab_ref[4:5, :] = tile_rows
    tab_ref[5:8, :] = jnp.zeros((3, LANES), I32)


def _route_meta(ri, *, tm):
    n = ri.shape[1]
    nb = n // LANES
    pos, tab = pl.pallas_call(
        functools.partial(_route_meta_kernel, tm=tm),
        out_shape=[jax.ShapeDtypeStruct((2, nb, LANES), I32), jax.ShapeDtypeStruct((8, LANES), I32)],
        compiler_params=pltpu.CompilerParams(vmem_limit_bytes=VMEM_LIMIT),
        name="route_meta",
    )(ri.reshape(2, nb, LANES))
    return pos.reshape(2 * n), tab


def _route_inv_kernel(pos_ref, lo_ref, hi_ref, code_ref, *, n, tm):
    def pad(p, carry):
        code_ref[p] = 2 * n + (p & (2 * tm - 1))
        return carry

    for e in range(N_EXPERTS):
        lax.fori_loop(lo_ref[e], hi_ref[e], pad, 0)
    lax.fori_loop(hi_ref[N_EXPERTS - 1], code_ref.shape[0], pad, 0)

    def body(t, carry):
        code_ref[pos_ref[t]] = 2 * t
        code_ref[pos_ref[n + t]] = 2 * t + 1
        return carry

    lax.fori_loop(0, n, body, 0, unroll=8)


def _route_inv(pos, pad_lo, pad_hi, *, n, rows, tm):
    smem = pl.BlockSpec(memory_space=pltpu.SMEM)
    return pl.pallas_call(
        functools.partial(_route_inv_kernel, n=n, tm=tm),
        in_specs=[smem, smem, smem],
        out_specs=smem,
        out_shape=jax.ShapeDtypeStruct((rows,), I32),
        name="route_inv",
    )(pos, pad_lo, pad_hi)


def _moe_ffn_kernel(te_ref, nr_ref, na_ref, code_ref, hn_hbm, wg_hbm, wu_hbm, wd_hbm, o_hbm,
                    xbuf, ybuf, xb_ref, acc_ref, wg_ring, wu_ring, wd_ring, gsem, ssem, wsem,
                    *, layer, n, tm, cw, roll, m_steps):
    i = pl.program_id(0)
    na = na_ref[0]
    xs = i % 2
    xo = 1 - xs
    nb = ybuf.shape[0]
    ys = i % nb
    yp = (i + nb - 1) % nb
    d = xb_ref.shape[1]
    nc = wg_hbm.shape[3] // cw
    rd = wg_ring.shape[0]
    ahead = rd - 1
    assert ahead < nc and nc % roll == 0 and nb in (1, 2)

    def ring_slot(tile, c):
        return (tile * nc + c) % rd

    def w_copies(e, c, ws):
        cs = pl.ds(pl.multiple_of(c * cw, cw), cw)
        return (pltpu.make_async_copy(wg_hbm.at[layer, e, :, cs], wg_ring.at[ws], wsem.at[ws]),
                pltpu.make_async_copy(wu_hbm.at[layer, e, :, cs], wu_ring.at[ws], wsem.at[ws]),
                pltpu.make_async_copy(wd_hbm.at[layer, e, cs, :], wd_ring.at[ws], wsem.at[ws]))

    def start_w(e, c, ws):
        for cp in w_copies(e, c, ws):
            cp.start()

    def wait_w(ws):
        for cp in w_copies(0, 0, ws):
            cp.wait()

    def gather_row(tile, sl, r):
        code = code_ref[tile * tm + r]
        tok = jnp.where(code < 2 * n, code >> 1, 0)
        pltpu.make_async_copy(hn_hbm.at[tok], xbuf.at[sl, r], gsem.at[sl]).start()

    def scatter_row(tile, sl, r, live):
        code = code_ref[tile * tm + r]
        dst = jnp.where(live, code, 2 * n + 2 * tm + r)
        pltpu.make_async_copy(ybuf.at[sl, r], o_hbm.at[dst], ssem.at[sl]).start()

    def wait_gather(sl):
        pltpu.make_async_copy(hn_hbm.at[pl.ds(0, tm)], xbuf.at[sl], gsem.at[sl]).wait()

    def wait_scatter(sl):
        pltpu.make_async_copy(ybuf.at[sl], o_hbm.at[pl.ds(0, tm)], ssem.at[sl]).wait()

    @pl.when(i == 0)
    def _():
        ybuf[...] = jnp.zeros(ybuf.shape, F32)
        for q in range(SPILL_TILES):
            cp = pltpu.make_async_copy(ybuf.at[0], o_hbm.at[pl.ds(2 * n + q * tm, tm)], ssem.at[0])
            cp.start()
            cp.wait()
        for c in range(ahead):
            start_w(te_ref[0], c, ring_slot(0, c))

        def body(r, carry):
            gather_row(0, 0, r)
            return carry

        lax.fori_loop(0, tm, body, 0, unroll=8)

    def tile_body(m):
        xb_ref[0:m] = xbuf[xs, 0:m].reshape(m, d).astype(BF16)
        e = te_ref[i]
        nxt = jnp.minimum(i + 1, na - 1)
        e_nxt = te_ref[nxt]
        prev = jnp.maximum(i - 1, 0)
        live = i > 0
        acc_ref[0:m] = jnp.zeros((m, d), F32)
        rpc = tm // nc

        def chunks(it, carry):
            for u in range(roll):
                c = it * roll + u
                ws = ring_slot(i, c)
                wait_w(ws)
                ca = c + ahead
                wrap = ca >= nc
                start_w(jnp.where(wrap, e_nxt, e), jnp.where(wrap, ca - nc, ca), ring_slot(i, ca))
                x = xb_ref[0:m]
                a = (_silu(_dot(x, wg_ring[ws].astype(BF16))) * _dot(x, wu_ring[ws].astype(BF16))).astype(BF16)
                acc_ref[0:m] += _dot(a, wd_ring[ws].astype(BF16))
                for q in range(rpc):
                    gather_row(nxt, xo, c * rpc + q)
                    scatter_row(prev, yp, c * rpc + q, live)
            return carry

        lax.fori_loop(0, nc // roll, chunks, 0)
        for r in range(nc * rpc, tm):
            gather_row(nxt, xo, r)
            scatter_row(prev, yp, r, live)

        @pl.when(i >= nb - 1)
        def _():
            wait_scatter(ys)

        ybuf[ys, 0:m] = acc_ref[0:m].reshape(m, d // LANES, LANES)

    @pl.when(i < na)
    def _():
        wait_gather(xs)
        real = nr_ref[i]
        lo = 0
        for m in m_steps:
            @pl.when(jnp.logical_and(real > lo, real <= m))
            def _(m=m):
                tile_body(m)

            lo = m

    @pl.when(i == na)
    def _():
        def body(r, carry):
            scatter_row(na - 1, yp, r, True)
            return carry

        lax.fori_loop(0, tm, body, 0, unroll=8)
        wait_scatter(yp)
        if nb == 2:
            wait_scatter(1 - yp)
        wait_gather(xs)
        for c in range(ahead):
            wait_w(ring_slot(na, c))


def _moe_ffn(hn, tile_expert, tile_rows, n_active, code, wg, wu, wd, *, layer, tm, cw, ring, roll, ybufs, m_steps):
    n, ncb, _ = hn.shape
    d = ncb * LANES
    rows = code.shape[0]
    hbm = pl.BlockSpec(memory_space=pl.ANY)
    return pl.pallas_call(
        functools.partial(_moe_ffn_kernel, layer=layer, n=n, tm=tm, cw=cw, roll=roll, m_steps=m_steps),
        grid_spec=pltpu.PrefetchScalarGridSpec(
            num_scalar_prefetch=4,
            grid=(rows // tm + 1,),
            in_specs=[hbm, hbm, hbm, hbm],
            out_specs=hbm,
            scratch_shapes=[
                pltpu.VMEM((2, tm, ncb, LANES), F32),
                pltpu.VMEM((ybufs, tm, ncb, LANES), F32),
                pltpu.VMEM((tm, d), BF16),
                pltpu.VMEM((tm, d), F32),
                pltpu.VMEM((ring, d, cw), F32),
                pltpu.VMEM((ring, d, cw), F32),
                pltpu.VMEM((ring, cw, d), F32),
                pltpu.SemaphoreType.DMA((2,)),
                pltpu.SemaphoreType.DMA((2,)),
                pltpu.SemaphoreType.DMA((ring,)),
            ],
        ),
        out_shape=jax.ShapeDtypeStruct((2 * n + SPILL_TILES * tm, ncb, LANES), F32),
        compiler_params=_cparams(("arbitrary",)),
        name="moe_ffn",
    )(tile_expert, tile_rows, n_active, code, hn, wg, wu, wd)


def _moe_final_kernel(h_ref, g_ref, gf_ref, y_ref, o_ref):
    o_ref[...] = _rms(_moe_combine(h_ref[...], g_ref, y_ref), gf_ref[...])


def _moe_final(h, gates_t, y, gf, *, td):
    n, d = h.shape
    ncb = d // LANES
    return pl.pallas_call(
        _moe_final_kernel,
        grid=(n // td,),
        in_specs=[
            pl.BlockSpec((td, d), lambda i: (i, 0)),
            pl.BlockSpec((td, 2), lambda i: (i, 0)),
            pl.BlockSpec((1, d), lambda i: (0, 0)),
            pl.BlockSpec((td, 2, ncb, LANES), lambda i: (i, 0, 0, 0)),
        ],
        out_specs=pl.BlockSpec((td, d), lambda i: (i, 0)),
        out_shape=jax.ShapeDtypeStruct((n, d), F32),
        compiler_params=_cparams(("arbitrary",)),
        name="moe_add",
    )(h, gates_t, gf.reshape(1, d), y.reshape(y.shape[0] // 2, 2, ncb, LANES))


MIX_TILE = 1024
MIX_SUB = 256
FFN_TM = 1024
FFN_SUB = 256
MOE_TM = 1024
MOE_CW = 256
MOE_RING = 4
MOE_ROLL = 1
MOE_YBUFS = 2
MOE_M_STEPS = (256, 512, 768, 1024)
ADD_TD = 1024


def _moe_rows(n, tm):
    return ((2 * n + N_EXPERTS * (tm - 1)) // tm) * tm


def kernel(x, norm_mix_e, w_in_e, sgu_norm, w_spatial, b_spatial, w_pool, pool_scale, w_out_e, norm_ffn_e, w_gate_d, w_up_d, w_down_d, norm_mix_o, w_pw1, b_pw1, w_dw, b_dw, ln_g, ln_b, w_pw2, b_pw2, norm_ffn_o, w_router, w_gate_x, w_up_x, w_down_x, norm_final):
    b, s, d = x.shape
    n = b * s
    depth = norm_mix_e.shape[0] + norm_mix_o.shape[0]
    if depth % 2 == 1:
        raise NotImplementedError("trunk depth must end on an odd (MoE) layer")
    rows = _moe_rows(n, MOE_TM)
    h = x
    moe = None
    for layer in range(depth):
        i = layer // 2
        if layer % 2 == 0:
            h = _even_mixer(h, norm_mix_e[i], w_in_e[i], sgu_norm[i], w_spatial[i], b_spatial[i], w_pool[i],
                            pool_scale[i], w_out_e[i], tile=MIX_TILE, sub=MIX_SUB, moe=moe)
            moe = None
            h = _dense_ffn(h.reshape(n, d), norm_ffn_e[i], w_gate_d[i], w_up_d[i], w_down_d[i],
                           tm=FFN_TM, sub=FFN_SUB).reshape(b, s, d)
        else:
            h, hn, ri, rg = _odd_mixer(h, norm_mix_o[i], w_pw1[i], b_pw1[i], w_dw[i], b_dw[i], ln_g[i], ln_b[i],
                                       w_pw2[i], b_pw2[i], norm_ffn_o[i], w_router[i], tile=MIX_TILE)
            pos, tab = _route_meta(ri, tm=MOE_TM)
            code = _route_inv(pos, tab[2, :N_EXPERTS], tab[3, :N_EXPERTS], n=n, rows=rows, tm=MOE_TM)
            nt = rows // MOE_TM
            ys = _moe_ffn(hn, tab[0, :nt], tab[4, :nt], tab[1, :1], code, w_gate_x, w_up_x, w_down_x,
                          layer=i, tm=MOE_TM, cw=MOE_CW, ring=MOE_RING, roll=MOE_ROLL, ybufs=MOE_YBUFS,
                          m_steps=MOE_M_STEPS)
            moe = (rg.T, ys)
    return _moe_final(h.reshape(n, d), *moe, norm_final, td=ADD_TD).reshape(b, s, d)
```

```python
import functools

import jax
import jax.numpy as jnp
from jax import lax
from jax.experimental import pallas as pl
from jax.experimental.pallas import tpu as pltpu

F32 = jnp.float32
BF16 = jnp.bfloat16
I32 = jnp.int32

EPS = 1e-6
CHUNK = 128
A_HEADS = 8
POOL_WINDOWS = (2, 4, 8, 16)
POOL_HALO = 16
CONV_K = 31
CONV_HALO = 32
N_EXPERTS = 8
SPILL_TILES = 3
LANES = 128
VMEM_LIMIT = 60 * 1024 * 1024


def _cparams(sem):
    return pltpu.CompilerParams(dimension_semantics=sem, vmem_limit_bytes=VMEM_LIMIT)


def _rms(x, g):
    return x * lax.rsqrt(jnp.mean(x * x, axis=-1, keepdims=True) + EPS) * g


def _gelu_tanh(x):
    c = 0.7978845608028654
    return 0.5 * x * (1.0 + jnp.tanh(c * (x + 0.044715 * (x * x * x))))


def _silu(x):
    return x * jax.nn.sigmoid(x)


def _dot(a, b):
    return jnp.dot(a, b, preferred_element_type=F32)


def _moe_combine(h, g_ref, y_ref):
    g = g_ref[...]
    return h + g[:, 0:1] * y_ref[:, 0].reshape(h.shape) + g[:, 1:2] * y_ref[:, 1].reshape(h.shape)


def _even_mixer_kernel(*refs, tile, sub, combine):
    if combine:
        h_ref, g_ref, y_ref = refs[:3]
        refs = refs[3:]
    else:
        h_ref = refs[0]
        refs = refs[1:]
    gn_ref, win_ref, gv_ref, ws_ref, bst_ref, wp_ref, ps_ref, wout_ref, o_ref, zs_ref, cat_ref = refs
    s = pl.program_id(1)
    d = h_ref.shape[-1]
    aw = d // 2
    nc = sub // CHUNK
    row = lax.broadcasted_iota(I32, (CHUNK, CHUNK), 0)
    col = lax.broadcasted_iota(I32, (CHUNK, CHUNK), 1)
    causal = row >= col
    lane = lax.broadcasted_iota(I32, (CHUNK, LANES), 1)
    lo = lane < (LANES // 2)

    @pl.when(s == 0)
    def _():
        zs_ref[0:POOL_HALO, :] = jnp.zeros((POOL_HALO, aw), F32)

    @pl.when(s > 0)
    def _():
        zs_ref[0:POOL_HALO, :] = zs_ref[tile:tile + POOL_HALO, :]

    for r0 in range(0, tile, sub):
        rows = slice(r0, r0 + sub)
        h = h_ref[0, rows, :]
        if combine:
            h = _moe_combine(h, g_ref.at[rows], y_ref.at[rows])
        hn = _rms(h, gn_ref[...]).astype(BF16)
        proj = _dot(hn, win_ref[...])
        u = _gelu_tanh(proj[:, :aw])
        v = _rms(_gelu_tanh(proj[:, aw:2 * aw]), gv_ref[...])
        z = proj[:, 2 * aw:]

        for j in range(aw // LANES):
            w_a = jnp.where(causal, ws_ref[2 * j], 0).astype(BF16)
            w_b = jnp.where(causal, ws_ref[2 * j + 1], 0).astype(BF16)
            lhs = jnp.concatenate([w_a, w_b], axis=1)
            cols = []
            for c in range(nc):
                vb = v[c * CHUNK:(c + 1) * CHUNK, j * LANES:(j + 1) * LANES]
                cols.append(jnp.concatenate([jnp.where(lo, vb, 0.0), jnp.where(lo, 0.0, vb)], axis=0))
            rhs = jnp.concatenate(cols, axis=1).astype(BF16)
            mixed = _dot(lhs, rhs)
            bias = jnp.where(lo, bst_ref[:, 2 * j:2 * j + 1], bst_ref[:, 2 * j + 1:2 * j + 2])
            for c in range(nc):
                ub = u[c * CHUNK:(c + 1) * CHUNK, j * LANES:(j + 1) * LANES]
                a_out = ub * (mixed[:, c * LANES:(c + 1) * LANES] + bias)
                cat_ref[r0 + c * CHUNK:r0 + (c + 1) * CHUNK, j * LANES:(j + 1) * LANES] = a_out.astype(BF16)

        zs_ref[POOL_HALO + r0:POOL_HALO + r0 + sub, :] = z
        pos = s * tile + r0 + 1 + lax.broadcasted_iota(I32, (sub, LANES), 0)
        for g, win in enumerate(POOL_WINDOWS):
            gs = slice(g * LANES, (g + 1) * LANES)
            acc = zs_ref[POOL_HALO + r0:POOL_HALO + r0 + sub, gs]
            for i in range(1, win):
                acc = acc + zs_ref[POOL_HALO + r0 - i:POOL_HALO + r0 - i + sub, gs]
            cnt = jnp.minimum(pos, win).astype(F32)
            mixed = acc / cnt - z[:, gs]
            y = _dot(mixed.astype(BF16), wp_ref[g]) * ps_ref[:, gs]
            cat_ref[rows, aw + g * LANES:aw + (g + 1) * LANES] = y.astype(BF16)

        o_ref[0, rows, :] = h + _dot(cat_ref[rows, :], wout_ref[...])


def _even_mixer(h, gn, w_in, gv, ws, bs, wp, ps, w_out, *, tile, sub, moe=None):
    b, s, d = h.shape
    aw = d // 2
    nst = s // tile
    ncb = d // LANES
    const2 = lambda *_: (0, 0)
    const3 = lambda *_: (0, 0, 0)
    moe_specs, moe_args = [], []
    if moe is not None:
        gates_t, y = moe
        moe_specs = [pl.BlockSpec((tile, 2), lambda i, j: (i * nst + j, 0)),
                     pl.BlockSpec((tile, 2, ncb, LANES), lambda i, j: (i * nst + j, 0, 0, 0))]
        moe_args = [gates_t, y.reshape(y.shape[0] // 2, 2, ncb, LANES)]
    return pl.pallas_call(
        functools.partial(_even_mixer_kernel, tile=tile, sub=sub, combine=moe is not None),
        grid=(b, nst),
        in_specs=[
            pl.BlockSpec((1, tile, d), lambda i, j: (i, j, 0)),
            *moe_specs,
            pl.BlockSpec((1, d), const2),
            pl.BlockSpec((d, 3 * aw), const2),
            pl.BlockSpec((1, aw), const2),
            pl.BlockSpec((A_HEADS, CHUNK, CHUNK), const3),
            pl.BlockSpec((CHUNK, A_HEADS), const2),
            pl.BlockSpec((len(POOL_WINDOWS), LANES, LANES), const3),
            pl.BlockSpec((1, aw), const2),
            pl.BlockSpec((d, d), const2),
        ],
        out_specs=pl.BlockSpec((1, tile, d), lambda i, j: (i, j, 0)),
        out_shape=jax.ShapeDtypeStruct(h.shape, F32),
        scratch_shapes=[pltpu.VMEM((tile + POOL_HALO, aw), F32), pltpu.VMEM((tile, d), BF16)],
        compiler_params=_cparams(("arbitrary", "arbitrary")),
        name="even_mixer",
    )(h, *moe_args, gn.reshape(1, d), w_in.astype(BF16), gv.reshape(1, aw), ws, bs.T, wp.astype(BF16),
      ps.reshape(1, aw), w_out.astype(BF16))


def _swiglu_hidden(x, wg_ref, wu_ref, a_ref, *, sub):
    for c in range(a_ref.shape[1] // sub):
        cs = slice(c * sub, (c + 1) * sub)
        a_ref[:, cs] = (_silu(_dot(x, wg_ref[:, cs])) * _dot(x, wu_ref[:, cs])).astype(BF16)


def _dense_ffn_kernel(h_ref, gn_ref, wg_ref, wu_ref, wd_ref, o_ref, a_ref, *, sub):
    h = h_ref[...]
    x = _rms(h, gn_ref[...]).astype(BF16)
    _swiglu_hidden(x, wg_ref, wu_ref, a_ref, sub=sub)
    o_ref[...] = h + _dot(a_ref[...], wd_ref[...])


def _dense_ffn(h, gn, wg, wu, wd, *, tm, sub):
    n, d = h.shape
    f = wg.shape[1]
    const2 = lambda i: (0, 0)
    return pl.pallas_call(
        functools.partial(_dense_ffn_kernel, sub=sub),
        grid=(n // tm,),
        in_specs=[
            pl.BlockSpec((tm, d), lambda i: (i, 0)),
            pl.BlockSpec((1, d), const2),
            pl.BlockSpec((d, f), const2),
            pl.BlockSpec((d, f), const2),
            pl.BlockSpec((f, d), const2),
        ],
        out_specs=pl.BlockSpec((tm, d), lambda i: (i, 0)),
        out_shape=jax.ShapeDtypeStruct((n, d), F32),
        scratch_shapes=[pltpu.VMEM((tm, f), BF16)],
        compiler_params=_cparams(("arbitrary",)),
        name="dense_ffn",
    )(h, gn.reshape(1, d), wg.astype(BF16), wu.astype(BF16), wd.astype(BF16))


def _split3(x):
    hi = x.astype(BF16)
    lo = (x - hi.astype(F32)).astype(BF16)
    return hi, lo


def _odd_mixer_kernel(h_ref, gn_ref, w1_ref, b1_ref, wdw_ref, bdw_ref, lg_ref, lb_ref, w2_ref, b2_ref,
                      gf_ref, wr_ref, o_ref, hn_ref, ri_ref, rg_ref, xs_ref, ys_ref, *, tile):
    s = pl.program_id(1)
    d = h_ref.shape[-1]
    ncb = d // LANES
    h = h_ref[0]
    hn = _rms(h, gn_ref[...]).astype(BF16)
    a = _dot(hn, w1_ref[...]) + b1_ref[...]
    x = a[:, :d] * jax.nn.sigmoid(a[:, d:])

    @pl.when(s == 0)
    def _():
        xs_ref[:, 0:CONV_HALO, :] = jnp.zeros((ncb, CONV_HALO, LANES), F32)

    @pl.when(s > 0)
    def _():
        xs_ref[:, 0:CONV_HALO, :] = xs_ref[:, tile:tile + CONV_HALO, :]

    for cb in range(ncb):
        xs_ref[cb, CONV_HALO:CONV_HALO + tile, :] = x[:, cb * LANES:(cb + 1) * LANES]

    rb = 128
    base = CONV_HALO - (CONV_K - 1)

    def conv_block(cb, carry):
        for r0 in range(0, tile, rb):
            acc = wdw_ref[cb, 0:1, :] * xs_ref[cb, r0 + base:r0 + base + rb, :]
            for k in range(1, CONV_K):
                acc = acc + wdw_ref[cb, k:k + 1, :] * xs_ref[cb, r0 + base + k:r0 + base + k + rb, :]
            ys_ref[cb, r0:r0 + rb, :] = acc
        return carry

    lax.fori_loop(0, ncb, conv_block, 0)
    y = jnp.concatenate([ys_ref[cb] for cb in range(ncb)], axis=1) + bdw_ref[...]

    mu = jnp.mean(y, axis=-1, keepdims=True)
    yc = y - mu
    var = jnp.mean(yc * yc, axis=-1, keepdims=True)
    yn = _silu(yc * lax.rsqrt(var + EPS) * lg_ref[...] + lb_ref[...])
    h2 = h + _dot(yn.astype(BF16), w2_ref[...]) + b2_ref[...]
    o_ref[0] = h2

    hf = _rms(h2, gf_ref[...])
    hn_ref[...] = hf.reshape(tile, d // LANES, LANES)
    x_hi, x_lo = _split3(hf)
    w_hi, w_lo = _split3(wr_ref[...])
    nt = (((1,), (1,)), ((), ()))
    logits = (lax.dot_general(w_hi, x_hi, nt, preferred_element_type=F32)
              + lax.dot_general(w_lo, x_hi, nt, preferred_element_type=F32)
              + lax.dot_general(w_hi, x_lo, nt, preferred_element_type=F32))
    eidx = lax.broadcasted_iota(I32, logits.shape, 0).astype(F32)
    m1 = jnp.max(logits, axis=0, keepdims=True)
    i1 = jnp.min(jnp.where(logits == m1, eidx, float(N_EXPERTS)), axis=0, keepdims=True)
    rest = jnp.where(eidx == i1, -jnp.inf, logits)
    m2 = jnp.max(rest, axis=0, keepdims=True)
    i2 = jnp.min(jnp.where(rest == m2, eidx, float(N_EXPERTS)), axis=0, keepdims=True)
    e2 = jnp.exp(m2 - m1)
    den = 1.0 + e2
    ri_ref[...] = jnp.concatenate([i1, i2], axis=0).astype(I32)
    rg_ref[...] = jnp.concatenate([1.0 / den, e2 / den], axis=0)


def _odd_mixer(h, gn, w1, b1, wdw, bdw, lg, lb, w2, b2, gf, wr, *, tile):
    b, s, d = h.shape
    n = b * s
    ncb = d // LANES
    nst = s // tile
    const2 = lambda *_: (0, 0)
    const3 = lambda *_: (0, 0, 0)
    wdw_b = jnp.transpose(wdw.reshape(CONV_K, ncb, LANES), (1, 0, 2))
    return pl.pallas_call(
        functools.partial(_odd_mixer_kernel, tile=tile),
        grid=(b, nst),
        in_specs=[
            pl.BlockSpec((1, tile, d), lambda i, j: (i, j, 0)),
            pl.BlockSpec((1, d), const2),
            pl.BlockSpec((d, 2 * d), const2),
            pl.BlockSpec((1, 2 * d), const2),
            pl.BlockSpec((ncb, CONV_K, LANES), const3),
            pl.BlockSpec((1, d), const2),
            pl.BlockSpec((1, d), const2),
            pl.BlockSpec((1, d), const2),
            pl.BlockSpec((d, d), const2),
            pl.BlockSpec((1, d), const2),
            pl.BlockSpec((1, d), const2),
            pl.BlockSpec((N_EXPERTS, d), const2),
        ],
        out_specs=[
            pl.BlockSpec((1, tile, d), lambda i, j: (i, j, 0)),
            pl.BlockSpec((tile, ncb, LANES), lambda i, j: (i * nst + j, 0, 0)),
            pl.BlockSpec((2, tile), lambda i, j: (0, i * nst + j)),
            pl.BlockSpec((2, tile), lambda i, j: (0, i * nst + j)),
        ],
        out_shape=[
            jax.ShapeDtypeStruct(h.shape, F32),
            jax.ShapeDtypeStruct((n, ncb, LANES), F32),
            jax.ShapeDtypeStruct((2, n), I32),
            jax.ShapeDtypeStruct((2, n), F32),
        ],
        scratch_shapes=[pltpu.VMEM((ncb, tile + CONV_HALO, LANES), F32), pltpu.VMEM((ncb, tile, LANES), F32)],
        compiler_params=_cparams(("arbitrary", "arbitrary")),
        name="odd_mixer",
    )(h, gn.reshape(1, d), w1.astype(BF16), b1.reshape(1, 2 * d), wdw_b, bdw.reshape(1, d),
      lg.reshape(1, d), lb.reshape(1, d), w2.astype(BF16), b2.reshape(1, d), gf.reshape(1, d), wr.T)


def _route_meta_kernel(ri_ref, pos_ref, tab_ref, *, tm):
    ri = ri_ref[...]
    i0, i1 = ri[0], ri[1]
    nb = i0.shape[0]
    shift = tm.bit_length() - 1
    r = lax.broadcasted_iota(I32, (LANES, LANES), 0)
    c = lax.broadcasted_iota(I32, (LANES, LANES), 1)
    upper = (r < c).astype(BF16)
    rb = lax.broadcasted_iota(I32, (nb, nb), 0)
    cb = lax.broadcasted_iota(I32, (nb, nb), 1)
    lower = (cb < rb).astype(BF16)
    lane = lax.broadcasted_iota(I32, (1, LANES), 1)
    pos0 = jnp.zeros(i0.shape, I32)
    pos1 = jnp.zeros(i0.shape, I32)
    goff = jnp.zeros((1, 1), I32)
    tile_expert = jnp.zeros((1, LANES), I32)
    tile_rows = jnp.zeros((1, LANES), I32)
    pad_lo = jnp.zeros((1, LANES), I32)
    pad_hi = jnp.zeros((1, LANES), I32)
    for e in range(N_EXPERTS):
        m = jnp.logical_or(i0 == e, i1 == e).astype(F32)
        within = _dot(m.astype(BF16), upper)
        tot = jnp.broadcast_to(jnp.sum(m, axis=1, keepdims=True), (nb, LANES))
        blk = _dot(lower, tot.astype(BF16))
        cnt = jnp.sum(tot[:, 0:1], axis=0, keepdims=True).astype(I32)
        dest = goff + (blk + within).astype(I32)
        pos0 = jnp.where(i0 == e, dest, pos0)
        pos1 = jnp.where(i1 == e, dest, pos1)
        pad_lo = jnp.where(lane == e, goff + cnt, pad_lo)
        tile_start = lane << shift
        in_group = tile_start >= goff
        tile_rows = jnp.where(in_group, jnp.clip(goff + cnt - tile_start, 0, tm), tile_rows)
        goff = goff + (((cnt + (tm - 1)) >> shift) << shift)
        pad_hi = jnp.where(lane == e, goff, pad_hi)
        tile_expert = tile_expert + ((lane << shift) >= goff).astype(I32)
    pos_ref[0] = pos0
    pos_ref[1] = pos1
    tab_ref[0:1, :] = jnp.minimum(tile_expert, N_EXPERTS - 1)
    tab_ref[1:2, :] = jnp.broadcast_to(goff >> shift, (1, LANES))
    tab_ref[2:3, :] = pad_lo
    tab_ref[3:4, :] = pad_hi
    tab_ref[4:5, :] = tile_rows
    tab_ref[5:8, :] = jnp.zeros((3, LANES), I32)


def _route_meta(ri, *, tm):
    n = ri.shape[1]
    nb = n // LANES
    pos, tab = pl.pallas_call(
        functools.partial(_route_meta_kernel, tm=tm),
        out_shape=[jax.ShapeDtypeStruct((2, nb, LANES), I32), jax.ShapeDtypeStruct((8, LANES), I32)],
        compiler_params=pltpu.CompilerParams(vmem_limit_bytes=VMEM_LIMIT),
        name="route_meta",
    )(ri.reshape(2, nb, LANES))
    return pos.reshape(2 * n), tab


def _route_inv_kernel(pos_ref, lo_ref, hi_ref, code_ref, *, n, tm):
    sh = 3

    def pad_block(j, base):
        for q in range(1 << sh):
            p = base + (j << sh) + q
            code_ref[p] = 2 * n + (p & (2 * tm - 1))
        return base

    def pad_range(lo, hi):
        lo = (lo >> sh) << sh
        lax.fori_loop(0, (hi - lo) >> sh, pad_block, lo)

    for e in range(N_EXPERTS):
        pad_range(lo_ref[e], hi_ref[e])
    pad_range(hi_ref[N_EXPERTS - 1], code_ref.shape[0])

    def body(t, carry):
        code_ref[pos_ref[t]] = 2 * t
        code_ref[pos_ref[n + t]] = 2 * t + 1
        return carry

    lax.fori_loop(0, n, body, 0, unroll=8)


def _route_inv(pos, pad_lo, pad_hi, *, n, rows, tm):
    smem = pl.BlockSpec(memory_space=pltpu.SMEM)
    return pl.pallas_call(
        functools.partial(_route_inv_kernel, n=n, tm=tm),
        in_specs=[smem, smem, smem],
        out_specs=smem,
        out_shape=jax.ShapeDtypeStruct((rows,), I32),
        name="route_inv",
    )(pos, pad_lo, pad_hi)


def _moe_ffn_kernel(te_ref, nr_ref, na_ref, code_ref, hn_hbm, wg_hbm, wu_hbm, wd_hbm, o_hbm,
                    xbuf, ybuf, xb_ref, acc_ref, wg_ring, wu_ring, wd_ring, gsem, ssem, wsem,
                    *, layer, n, tm, cw, roll, m_steps):
    i = pl.program_id(0)
    na = na_ref[0]
    xs = i % 2
    xo = 1 - xs
    nb = ybuf.shape[0]
    ys = i % nb
    yp = (i + nb - 1) % nb
    d = xb_ref.shape[1]
    nc = wg_hbm.shape[3] // cw
    rd = wg_ring.shape[0]
    ahead = rd - 1
    assert ahead < nc and nc % roll == 0 and nb in (1, 2)

    def ring_slot(tile, c):
        return (tile * nc + c) % rd

    def w_copies(e, c, ws):
        cs = pl.ds(pl.multiple_of(c * cw, cw), cw)
        return (pltpu.make_async_copy(wg_hbm.at[layer, e, :, cs], wg_ring.at[ws], wsem.at[ws]),
                pltpu.make_async_copy(wu_hbm.at[layer, e, :, cs], wu_ring.at[ws], wsem.at[ws]),
                pltpu.make_async_copy(wd_hbm.at[layer, e, cs, :], wd_ring.at[ws], wsem.at[ws]))

    def start_w(e, c, ws):
        for cp in w_copies(e, c, ws):
            cp.start()

    def wait_w(ws):
        for cp in w_copies(0, 0, ws):
            cp.wait()

    def gather_row(tile, sl, r):
        code = code_ref[tile * tm + r]
        tok = jnp.where(code < 2 * n, code >> 1, 0)
        pltpu.make_async_copy(hn_hbm.at[tok], xbuf.at[sl, r], gsem.at[sl]).start()

    def scatter_row(tile, sl, r, live):
        code = code_ref[tile * tm + r]
        dst = jnp.where(live, code, 2 * n + 2 * tm + r)
        pltpu.make_async_copy(ybuf.at[sl, r], o_hbm.at[dst], ssem.at[sl]).start()

    def wait_gather(sl):
        pltpu.make_async_copy(hn_hbm.at[pl.ds(0, tm)], xbuf.at[sl], gsem.at[sl]).wait()

    def wait_scatter(sl):
        pltpu.make_async_copy(ybuf.at[sl], o_hbm.at[pl.ds(0, tm)], ssem.at[sl]).wait()

    @pl.when(i == 0)
    def _():
        ybuf[...] = jnp.zeros(ybuf.shape, F32)
        for q in range(SPILL_TILES):
            cp = pltpu.make_async_copy(ybuf.at[0], o_hbm.at[pl.ds(2 * n + q * tm, tm)], ssem.at[0])
            cp.start()
            cp.wait()
        for c in range(ahead):
            start_w(te_ref[0], c, ring_slot(0, c))

        def body(r, carry):
            gather_row(0, 0, r)
            return carry

        lax.fori_loop(0, tm, body, 0, unroll=8)

    def tile_body(m):
        xb_ref[0:m] = xbuf[xs, 0:m].reshape(m, d).astype(BF16)
        e = te_ref[i]
        nxt = jnp.minimum(i + 1, na - 1)
        e_nxt = te_ref[nxt]
        prev = jnp.maximum(i - 1, 0)
        live = i > 0
        acc_ref[0:m] = jnp.zeros((m, d), F32)
        rpc = tm // nc

        def chunks(it, carry):
            for u in range(roll):
                c = it * roll + u
                ws = ring_slot(i, c)
                wait_w(ws)
                ca = c + ahead
                wrap = ca >= nc
                start_w(jnp.where(wrap, e_nxt, e), jnp.where(wrap, ca - nc, ca), ring_slot(i, ca))
                x = xb_ref[0:m]
                a = (_silu(_dot(x, wg_ring[ws].astype(BF16))) * _dot(x, wu_ring[ws].astype(BF16))).astype(BF16)
                acc_ref[0:m] += _dot(a, wd_ring[ws].astype(BF16))
                for q in range(rpc):
                    gather_row(nxt, xo, c * rpc + q)
                    scatter_row(prev, yp, c * rpc + q, live)
            return carry

        lax.fori_loop(0, nc // roll, chunks, 0)
        for r in range(nc * rpc, tm):
            gather_row(nxt, xo, r)
            scatter_row(prev, yp, r, live)

        @pl.when(i >= nb - 1)
        def _():
            wait_scatter(ys)

        ybuf[ys, 0:m] = acc_ref[0:m].reshape(m, d // LANES, LANES)

    @pl.when(i < na)
    def _():
        wait_gather(xs)
        real = nr_ref[i]
        lo = 0
        for m in m_steps:
            @pl.when(jnp.logical_and(real > lo, real <= m))
            def _(m=m):
                tile_body(m)

            lo = m

    @pl.when(i == na)
    def _():
        def body(r, carry):
            scatter_row(na - 1, yp, r, True)
            return carry

        lax.fori_loop(0, tm, body, 0, unroll=8)
        wait_scatter(yp)
        if nb == 2:
            wait_scatter(1 - yp)
        wait_gather(xs)
        for c in range(ahead):
            wait_w(ring_slot(na, c))


def _moe_ffn(hn, tile_expert, tile_rows, n_active, code, wg, wu, wd, *, layer, tm, cw, ring, roll, ybufs, m_steps):
    n, ncb, _ = hn.shape
    d = ncb * LANES
    rows = code.shape[0]
    hbm = pl.BlockSpec(memory_space=pl.ANY)
    return pl.pallas_call(
        functools.partial(_moe_ffn_kernel, layer=layer, n=n, tm=tm, cw=cw, roll=roll, m_steps=m_steps),
        grid_spec=pltpu.PrefetchScalarGridSpec(
            num_scalar_prefetch=4,
            grid=(rows // tm + 1,),
            in_specs=[hbm, hbm, hbm, hbm],
            out_specs=hbm,
            scratch_shapes=[
                pltpu.VMEM((2, tm, ncb, LANES), F32),
                pltpu.VMEM((ybufs, tm, ncb, LANES), F32),
                pltpu.VMEM((tm, d), BF16),
                pltpu.VMEM((tm, d), F32),
                pltpu.VMEM((ring, d, cw), F32),
                pltpu.VMEM((ring, d, cw), F32),
                pltpu.VMEM((ring, cw, d), F32),
                pltpu.SemaphoreType.DMA((2,)),
                pltpu.SemaphoreType.DMA((2,)),
                pltpu.SemaphoreType.DMA((ring,)),
            ],
        ),
        out_shape=jax.ShapeDtypeStruct((2 * n + SPILL_TILES * tm, ncb, LANES), F32),
        compiler_params=_cparams(("arbitrary",)),
        name="moe_ffn",
    )(tile_expert, tile_rows, n_active, code, hn, wg, wu, wd)


def _moe_final_kernel(h_ref, g_ref, gf_ref, y_ref, o_ref):
    o_ref[...] = _rms(_moe_combine(h_ref[...], g_ref, y_ref), gf_ref[...])


def _moe_final(h, gates_t, y, gf, *, td):
    n, d = h.shape
    ncb = d // LANES
    return pl.pallas_call(
        _moe_final_kernel,
        grid=(n // td,),
        in_specs=[
            pl.BlockSpec((td, d), lambda i: (i, 0)),
            pl.BlockSpec((td, 2), lambda i: (i, 0)),
            pl.BlockSpec((1, d), lambda i: (0, 0)),
            pl.BlockSpec((td, 2, ncb, LANES), lambda i: (i, 0, 0, 0)),
        ],
        out_specs=pl.BlockSpec((td, d), lambda i: (i, 0)),
        out_shape=jax.ShapeDtypeStruct((n, d), F32),
        compiler_params=_cparams(("arbitrary",)),
        name="moe_add",
    )(h, gates_t, gf.reshape(1, d), y.reshape(y.shape[0] // 2, 2, ncb, LANES))


MIX_TILE = 1024
MIX_SUB = 256
FFN_TM = 1024
FFN_SUB = 256
MOE_TM = 1024
MOE_CW = 256
MOE_RING = 4
MOE_ROLL = 1
MOE_YBUFS = 2
MOE_M_STEPS = (256, 512, 768, 1024)
ADD_TD = 1024


def _moe_rows(n, tm):
    return ((2 * n + N_EXPERTS * (tm - 1)) // tm) * tm


def kernel(x, norm_mix_e, w_in_e, sgu_norm, w_spatial, b_spatial, w_pool, pool_scale, w_out_e, norm_ffn_e, w_gate_d, w_up_d, w_down_d, norm_mix_o, w_pw1, b_pw1, w_dw, b_dw, ln_g, ln_b, w_pw2, b_pw2, norm_ffn_o, w_router, w_gate_x, w_up_x, w_down_x, norm_final):
    b, s, d = x.shape
    n = b * s
    depth = norm_mix_e.shape[0] + norm_mix_o.shape[0]
    if depth % 2 == 1:
        raise NotImplementedError("trunk depth must end on an odd (MoE) layer")
    rows = _moe_rows(n, MOE_TM)
    h = x
    moe = None
    for layer in range(depth):
        i = layer // 2
        if layer % 2 == 0:
            h = _even_mixer(h, norm_mix_e[i], w_in_e[i], sgu_norm[i], w_spatial[i], b_spatial[i], w_pool[i],
                            pool_scale[i], w_out_e[i], tile=MIX_TILE, sub=MIX_SUB, moe=moe)
            moe = None
            h = _dense_ffn(h.reshape(n, d), norm_ffn_e[i], w_gate_d[i], w_up_d[i], w_down_d[i],
                           tm=FFN_TM, sub=FFN_SUB).reshape(b, s, d)
        else:
            h, hn, ri, rg = _odd_mixer(h, norm_mix_o[i], w_pw1[i], b_pw1[i], w_dw[i], b_dw[i], ln_g[i], ln_b[i],
                                       w_pw2[i], b_pw2[i], norm_ffn_o[i], w_router[i], tile=MIX_TILE)
            pos, tab = _route_meta(ri, tm=MOE_TM)
            code = _route_inv(pos, tab[2, :N_EXPERTS], tab[3, :N_EXPERTS], n=n, rows=rows, tm=MOE_TM)
            nt = rows // MOE_TM
            ys = _moe_ffn(hn, tab[0, :nt], tab[4, :nt], tab[1, :1], code, w_gate_x, w_up_x, w_down_x,
                          layer=i, tm=MOE_TM, cw=MOE_CW, ring=MOE_RING, roll=MOE_ROLL, ybufs=MOE_YBUFS,
                          m_steps=MOE_M_STEPS)
            moe = (rg.T, ys)
    return _moe_final(h.reshape(n, d), *moe, norm_final, td=ADD_TD).reshape(b, s, d)
```

```python
import functools

import jax
import jax.numpy as jnp
from jax import lax
from jax.experimental import pallas as pl
from jax.experimental.pallas import tpu as pltpu

F32 = jnp.float32
BF16 = jnp.bfloat16
I32 = jnp.int32

EPS = 1e-6
CHUNK = 128
A_HEADS = 8
POOL_WINDOWS = (2, 4, 8, 16)
POOL_HALO = 16
CONV_K = 31
CONV_HALO = 32
N_EXPERTS = 8
SPILL_TILES = 3
LANES = 128
VMEM_LIMIT = 60 * 1024 * 1024


def _cparams(sem):
    return pltpu.CompilerParams(dimension_semantics=sem, vmem_limit_bytes=VMEM_LIMIT)


def _rms(x, g):
    return x * lax.rsqrt(jnp.mean(x * x, axis=-1, keepdims=True) + EPS) * g


def _gelu_tanh(x):
    c = 0.7978845608028654
    return 0.5 * x * (1.0 + jnp.tanh(c * (x + 0.044715 * (x * x * x))))


def _silu(x):
    return x * jax.nn.sigmoid(x)


def _dot(a, b):
    return jnp.dot(a, b, preferred_element_type=F32)


def _moe_combine(h, g_ref, y_ref):
    g = g_ref[...]
    return h + g[:, 0:1] * y_ref[:, 0].reshape(h.shape) + g[:, 1:2] * y_ref[:, 1].reshape(h.shape)


def _even_mixer_kernel(*refs, tile, sub, combine):
    if combine:
        h_ref, g_ref, y_ref = refs[:3]
        refs = refs[3:]
    else:
        h_ref = refs[0]
        refs = refs[1:]
    gn_ref, win_ref, gv_ref, ws_ref, bst_ref, wp_ref, ps_ref, wout_ref, o_ref, zs_ref, cat_ref = refs
    s = pl.program_id(1)
    d = h_ref.shape[-1]
    aw = d // 2
    nc = sub // CHUNK
    row = lax.broadcasted_iota(I32, (CHUNK, CHUNK), 0)
    col = lax.broadcasted_iota(I32, (CHUNK, CHUNK), 1)
    causal = row >= col
    lane = lax.broadcasted_iota(I32, (CHUNK, LANES), 1)
    lo = lane < (LANES // 2)

    @pl.when(s == 0)
    def _():
        zs_ref[0:POOL_HALO, :] = jnp.zeros((POOL_HALO, aw), F32)

    @pl.when(s > 0)
    def _():
        zs_ref[0:POOL_HALO, :] = zs_ref[tile:tile + POOL_HALO, :]

    for r0 in range(0, tile, sub):
        rows = slice(r0, r0 + sub)
        h = h_ref[0, rows, :]
        if combine:
            h = _moe_combine(h, g_ref.at[rows], y_ref.at[rows])
        hn = _rms(h, gn_ref[...]).astype(BF16)
        proj = _dot(hn, win_ref[...])
        u = _gelu_tanh(proj[:, :aw])
        v = _rms(_gelu_tanh(proj[:, aw:2 * aw]), gv_ref[...])
        z = proj[:, 2 * aw:]

        for j in range(aw // LANES):
            w_a = jnp.where(causal, ws_ref[2 * j], 0).astype(BF16)
            w_b = jnp.where(causal, ws_ref[2 * j + 1], 0).astype(BF16)
            lhs = jnp.concatenate([w_a, w_b], axis=1)
            cols = []
            for c in range(nc):
                vb = v[c * CHUNK:(c + 1) * CHUNK, j * LANES:(j + 1) * LANES]
                cols.append(jnp.concatenate([jnp.where(lo, vb, 0.0), jnp.where(lo, 0.0, vb)], axis=0))
            rhs = jnp.concatenate(cols, axis=1).astype(BF16)
            mixed = _dot(lhs, rhs)
            bias = jnp.where(lo, bst_ref[:, 2 * j:2 * j + 1], bst_ref[:, 2 * j + 1:2 * j + 2])
            for c in range(nc):
                ub = u[c * CHUNK:(c + 1) * CHUNK, j * LANES:(j + 1) * LANES]
                a_out = ub * (mixed[:, c * LANES:(c + 1) * LANES] + bias)
                cat_ref[r0 + c * CHUNK:r0 + (c + 1) * CHUNK, j * LANES:(j + 1) * LANES] = a_out.astype(BF16)

        zs_ref[POOL_HALO + r0:POOL_HALO + r0 + sub, :] = z
        pos = s * tile + r0 + 1 + lax.broadcasted_iota(I32, (sub, LANES), 0)
        for g, win in enumerate(POOL_WINDOWS):
            gs = slice(g * LANES, (g + 1) * LANES)
            acc = zs_ref[POOL_HALO + r0:POOL_HALO + r0 + sub, gs]
            for i in range(1, win):
                acc = acc + zs_ref[POOL_HALO + r0 - i:POOL_HALO + r0 - i + sub, gs]
            cnt = jnp.minimum(pos, win).astype(F32)
            mixed = acc / cnt - z[:, gs]
            y = _dot(mixed.astype(BF16), wp_ref[g]) * ps_ref[:, gs]
            cat_ref[rows, aw + g * LANES:aw + (g + 1) * LANES] = y.astype(BF16)

        o_ref[0, rows, :] = h + _dot(cat_ref[rows, :], wout_ref[...])


def _even_mixer(h, gn, w_in, gv, ws, bs, wp, ps, w_out, *, tile, sub, moe=None):
    b, s, d = h.shape
    aw = d // 2
    nst = s // tile
    ncb = d // LANES
    const2 = lambda *_: (0, 0)
    const3 = lambda *_: (0, 0, 0)
    moe_specs, moe_args = [], []
    if moe is not None:
        gates_t, y = moe
        moe_specs = [pl.BlockSpec((tile, 2), lambda i, j: (i * nst + j, 0)),
                     pl.BlockSpec((tile, 2, ncb, LANES), lambda i, j: (i * nst + j, 0, 0, 0))]
        moe_args = [gates_t, y.reshape(y.shape[0] // 2, 2, ncb, LANES)]
    return pl.pallas_call(
        functools.partial(_even_mixer_kernel, tile=tile, sub=sub, combine=moe is not None),
        grid=(b, nst),
        in_specs=[
            pl.BlockSpec((1, tile, d), lambda i, j: (i, j, 0)),
            *moe_specs,
            pl.BlockSpec((1, d), const2),
            pl.BlockSpec((d, 3 * aw), const2),
            pl.BlockSpec((1, aw), const2),
            pl.BlockSpec((A_HEADS, CHUNK, CHUNK), const3),
            pl.BlockSpec((CHUNK, A_HEADS), const2),
            pl.BlockSpec((len(POOL_WINDOWS), LANES, LANES), const3),
            pl.BlockSpec((1, aw), const2),
            pl.BlockSpec((d, d), const2),
        ],
        out_specs=pl.BlockSpec((1, tile, d), lambda i, j: (i, j, 0)),
        out_shape=jax.ShapeDtypeStruct(h.shape, F32),
        scratch_shapes=[pltpu.VMEM((tile + POOL_HALO, aw), F32), pltpu.VMEM((tile, d), BF16)],
        compiler_params=_cparams(("arbitrary", "arbitrary")),
        name="even_mixer",
    )(h, *moe_args, gn.reshape(1, d), w_in.astype(BF16), gv.reshape(1, aw), ws, bs.T, wp.astype(BF16),
      ps.reshape(1, aw), w_out.astype(BF16))


def _swiglu_hidden(x, wg_ref, wu_ref, a_ref, *, sub):
    for c in range(a_ref.shape[1] // sub):
        cs = slice(c * sub, (c + 1) * sub)
        a_ref[:, cs] = (_silu(_dot(x, wg_ref[:, cs])) * _dot(x, wu_ref[:, cs])).astype(BF16)


def _dense_ffn_kernel(h_ref, gn_ref, wg_ref, wu_ref, wd_ref, o_ref, a_ref, *, sub):
    h = h_ref[...]
    x = _rms(h, gn_ref[...]).astype(BF16)
    _swiglu_hidden(x, wg_ref, wu_ref, a_ref, sub=sub)
    o_ref[...] = h + _dot(a_ref[...], wd_ref[...])


def _dense_ffn(h, gn, wg, wu, wd, *, tm, sub):
    n, d = h.shape
    f = wg.shape[1]
    const2 = lambda i: (0, 0)
    return pl.pallas_call(
        functools.partial(_dense_ffn_kernel, sub=sub),
        grid=(n // tm,),
        in_specs=[
            pl.BlockSpec((tm, d), lambda i: (i, 0)),
            pl.BlockSpec((1, d), const2),
            pl.BlockSpec((d, f), const2),
            pl.BlockSpec((d, f), const2),
            pl.BlockSpec((f, d), const2),
        ],
        out_specs=pl.BlockSpec((tm, d), lambda i: (i, 0)),
        out_shape=jax.ShapeDtypeStruct((n, d), F32),
        scratch_shapes=[pltpu.VMEM((tm, f), BF16)],
        compiler_params=_cparams(("arbitrary",)),
        name="dense_ffn",
    )(h, gn.reshape(1, d), wg.astype(BF16), wu.astype(BF16), wd.astype(BF16))


def _split3(x):
    hi = x.astype(BF16)
    lo = (x - hi.astype(F32)).astype(BF16)
    return hi, lo


def _odd_mixer_kernel(h_ref, gn_ref, w1_ref, b1_ref, wdw_ref, bdw_ref, lg_ref, lb_ref, w2_ref, b2_ref,
                      gf_ref, wr_ref, o_ref, hn_ref, ri_ref, rg_ref, xs_ref, ys_ref, *, tile):
    s = pl.program_id(1)
    d = h_ref.shape[-1]
    ncb = d // LANES
    h = h_ref[0]
    hn = _rms(h, gn_ref[...]).astype(BF16)
    a = _dot(hn, w1_ref[...]) + b1_ref[...]
    x = a[:, :d] * jax.nn.sigmoid(a[:, d:])

    @pl.when(s == 0)
    def _():
        xs_ref[:, 0:CONV_HALO, :] = jnp.zeros((ncb, CONV_HALO, LANES), F32)

    @pl.when(s > 0)
    def _():
        xs_ref[:, 0:CONV_HALO, :] = xs_ref[:, tile:tile + CONV_HALO, :]

    for cb in range(ncb):
        xs_ref[cb, CONV_HALO:CONV_HALO + tile, :] = x[:, cb * LANES:(cb + 1) * LANES]

    rb = 128
    base = CONV_HALO - (CONV_K - 1)

    def conv_block(cb, carry):
        for r0 in range(0, tile, rb):
            acc = wdw_ref[cb, 0:1, :] * xs_ref[cb, r0 + base:r0 + base + rb, :]
            for k in range(1, CONV_K):
                acc = acc + wdw_ref[cb, k:k + 1, :] * xs_ref[cb, r0 + base + k:r0 + base + k + rb, :]
            ys_ref[cb, r0:r0 + rb, :] = acc
        return carry

    lax.fori_loop(0, ncb, conv_block, 0)
    y = jnp.concatenate([ys_ref[cb] for cb in range(ncb)], axis=1) + bdw_ref[...]

    mu = jnp.mean(y, axis=-1, keepdims=True)
    yc = y - mu
    var = jnp.mean(yc * yc, axis=-1, keepdims=True)
    yn = _silu(yc * lax.rsqrt(var + EPS) * lg_ref[...] + lb_ref[...])
    h2 = h + _dot(yn.astype(BF16), w2_ref[...]) + b2_ref[...]
    o_ref[0] = h2

    hf = _rms(h2, gf_ref[...])
    hn_ref[...] = hf.reshape(tile, d // LANES, LANES)
    x_hi, x_lo = _split3(hf)
    w_hi, w_lo = _split3(wr_ref[...])
    nt = (((1,), (1,)), ((), ()))
    logits = (lax.dot_general(w_hi, x_hi, nt, preferred_element_type=F32)
              + lax.dot_general(w_lo, x_hi, nt, preferred_element_type=F32)
              + lax.dot_general(w_hi, x_lo, nt, preferred_element_type=F32))
    eidx = lax.broadcasted_iota(I32, logits.shape, 0).astype(F32)
    m1 = jnp.max(logits, axis=0, keepdims=True)
    i1 = jnp.min(jnp.where(logits == m1, eidx, float(N_EXPERTS)), axis=0, keepdims=True)
    rest = jnp.where(eidx == i1, -jnp.inf, logits)
    m2 = jnp.max(rest, axis=0, keepdims=True)
    i2 = jnp.min(jnp.where(rest == m2, eidx, float(N_EXPERTS)), axis=0, keepdims=True)
    e2 = jnp.exp(m2 - m1)
    den = 1.0 + e2
    ri_ref[...] = jnp.concatenate([i1, i2], axis=0).astype(I32)
    rg_ref[...] = jnp.concatenate([1.0 / den, e2 / den], axis=0)


def _odd_mixer(h, gn, w1, b1, wdw, bdw, lg, lb, w2, b2, gf, wr, *, tile):
    b, s, d = h.shape
    n = b * s
    ncb = d // LANES
    nst = s // tile
    const2 = lambda *_: (0, 0)
    const3 = lambda *_: (0, 0, 0)
    wdw_b = jnp.transpose(wdw.reshape(CONV_K, ncb, LANES), (1, 0, 2))
    return pl.pallas_call(
        functools.partial(_odd_mixer_kernel, tile=tile),
        grid=(b, nst),
        in_specs=[
            pl.BlockSpec((1, tile, d), lambda i, j: (i, j, 0)),
            pl.BlockSpec((1, d), const2),
            pl.BlockSpec((d, 2 * d), const2),
            pl.BlockSpec((1, 2 * d), const2),
            pl.BlockSpec((ncb, CONV_K, LANES), const3),
            pl.BlockSpec((1, d), const2),
            pl.BlockSpec((1, d), const2),
            pl.BlockSpec((1, d), const2),
            pl.BlockSpec((d, d), const2),
            pl.BlockSpec((1, d), const2),
            pl.BlockSpec((1, d), const2),
            pl.BlockSpec((N_EXPERTS, d), const2),
        ],
        out_specs=[
            pl.BlockSpec((1, tile, d), lambda i, j: (i, j, 0)),
            pl.BlockSpec((tile, ncb, LANES), lambda i, j: (i * nst + j, 0, 0)),
            pl.BlockSpec((2, tile), lambda i, j: (0, i * nst + j)),
            pl.BlockSpec((2, tile), lambda i, j: (0, i * nst + j)),
        ],
        out_shape=[
            jax.ShapeDtypeStruct(h.shape, F32),
            jax.ShapeDtypeStruct((n, ncb, LANES), F32),
            jax.ShapeDtypeStruct((2, n), I32),
            jax.ShapeDtypeStruct((2, n), F32),
        ],
        scratch_shapes=[pltpu.VMEM((ncb, tile + CONV_HALO, LANES), F32), pltpu.VMEM((ncb, tile, LANES), F32)],
        compiler_params=_cparams(("arbitrary", "arbitrary")),
        name="odd_mixer",
    )(h, gn.reshape(1, d), w1.astype(BF16), b1.reshape(1, 2 * d), wdw_b, bdw.reshape(1, d),
      lg.reshape(1, d), lb.reshape(1, d), w2.astype(BF16), b2.reshape(1, d), gf.reshape(1, d), wr.T)


def _route_meta_kernel(ri_ref, pos_ref, tab_ref, *, tm):
    ri = ri_ref[...]
    i0, i1 = ri[0], ri[1]
    nb = i0.shape[0]
    shift = tm.bit_length() - 1
    r = lax.broadcasted_iota(I32, (LANES, LANES), 0)
    c = lax.broadcasted_iota(I32, (LANES, LANES), 1)
    upper = (r < c).astype(BF16)
    rb = lax.broadcasted_iota(I32, (nb, nb), 0)
    cb = lax.broadcasted_iota(I32, (nb, nb), 1)
    lower = (cb < rb).astype(BF16)
    lane = lax.broadcasted_iota(I32, (1, LANES), 1)
    pos0 = jnp.zeros(i0.shape, I32)
    pos1 = jnp.zeros(i0.shape, I32)
    goff = jnp.zeros((1, 1), I32)
    tile_expert = jnp.zeros((1, LANES), I32)
    tile_rows = jnp.zeros((1, LANES), I32)
    pad_lo = jnp.zeros((1, LANES), I32)
    pad_hi = jnp.zeros((1, LANES), I32)
    for e in range(N_EXPERTS):
        m = jnp.logical_or(i0 == e, i1 == e).astype(F32)
        within = _dot(m.astype(BF16), upper)
        tot = jnp.broadcast_to(jnp.sum(m, axis=1, keepdims=True), (nb, LANES))
        blk = _dot(lower, tot.astype(BF16))
        cnt = jnp.sum(tot[:, 0:1], axis=0, keepdims=True).astype(I32)
        dest = goff + (blk + within).astype(I32)
        pos0 = jnp.where(i0 == e, dest, pos0)
        pos1 = jnp.where(i1 == e, dest, pos1)
        pad_lo = jnp.where(lane == e, goff + cnt, pad_lo)
        tile_start = lane << shift
        in_group = tile_start >= goff
        tile_rows = jnp.where(in_group, jnp.clip(goff + cnt - tile_start, 0, tm), tile_rows)
        goff = goff + (((cnt + (tm - 1)) >> shift) << shift)
        pad_hi = jnp.where(lane == e, goff, pad_hi)
        tile_expert = tile_expert + ((lane << shift) >= goff).astype(I32)
    pos_ref[0] = pos0
    pos_ref[1] = pos1
    tab_ref[0:1, :] = jnp.minimum(tile_expert, N_EXPERTS - 1)
    tab_ref[1:2, :] = jnp.broadcast_to(goff >> shift, (1, LANES))
    tab_ref[2:3, :] = pad_lo
    tab_ref[3:4, :] = pad_hi
    tab_ref[4:5, :] = tile_rows
    tab_ref[5:8, :] = jnp.zeros((3, LANES), I32)


def _route_meta(ri, *, tm):
    n = ri.shape[1]
    nb = n // LANES
    pos, tab = pl.pallas_call(
        functools.partial(_route_meta_kernel, tm=tm),
        out_shape=[jax.ShapeDtypeStruct((2, nb, LANES), I32), jax.ShapeDtypeStruct((8, LANES), I32)],
        compiler_params=pltpu.CompilerParams(vmem_limit_bytes=VMEM_LIMIT),
        name="route_meta",
    )(ri.reshape(2, nb, LANES))
    return pos.reshape(2 * n), tab


def _route_inv_kernel(pos_ref, lo_ref, hi_ref, code_ref, *, n, tm):
    sh = 3

    def pad_block(j, base):
        for q in range(1 << sh):
            p = base + (j << sh) + q
            code_ref[p] = 2 * n + (p & (2 * tm - 1))
        return base

    def pad_range(lo, hi):
        lo = (lo >> sh) << sh
        lax.fori_loop(0, (hi - lo) >> sh, pad_block, lo)

    for e in range(N_EXPERTS):
        pad_range(lo_ref[e], hi_ref[e])
    pad_range(hi_ref[N_EXPERTS - 1], code_ref.shape[0])

    def body(t, carry):
        code_ref[pos_ref[t]] = 2 * t
        code_ref[pos_ref[n + t]] = 2 * t + 1
        return carry

    lax.fori_loop(0, n, body, 0, unroll=8)


def _route_inv(pos, pad_lo, pad_hi, *, n, rows, tm):
    smem = pl.BlockSpec(memory_space=pltpu.SMEM)
    return pl.pallas_call(
        functools.partial(_route_inv_kernel, n=n, tm=tm),
        in_specs=[smem, smem, smem],
        out_specs=smem,
        out_shape=jax.ShapeDtypeStruct((rows,), I32),
        name="route_inv",
    )(pos, pad_lo, pad_hi)


def _moe_ffn_kernel(te_ref, nr_ref, na_ref, code_ref, hn_hbm, wg_hbm, wu_hbm, wd_hbm, o_hbm,
                    xbuf, ybuf, xb_ref, acc_ref, wg_ring, wu_ring, wd_ring, gsem, ssem, wsem,
                    *, layer, n, tm, cw, roll, m_steps):
    i = pl.program_id(0)
    na = na_ref[0]
    xs = i % 2
    xo = 1 - xs
    nb = ybuf.shape[0]
    ys = i % nb
    yp = (i + nb - 1) % nb
    d = xb_ref.shape[1]
    nc = wg_hbm.shape[3] // cw
    rd = wg_ring.shape[0]
    ahead = rd - 1
    assert ahead < nc and nc % roll == 0 and nb in (1, 2)

    def ring_slot(tile, c):
        return (tile * nc + c) % rd

    def w_copies(e, c, ws):
        cs = pl.ds(pl.multiple_of(c * cw, cw), cw)
        return (pltpu.make_async_copy(wg_hbm.at[layer, e, :, cs], wg_ring.at[ws], wsem.at[ws]),
                pltpu.make_async_copy(wu_hbm.at[layer, e, :, cs], wu_ring.at[ws], wsem.at[ws]),
                pltpu.make_async_copy(wd_hbm.at[layer, e, cs, :], wd_ring.at[ws], wsem.at[ws]))

    def start_w(e, c, ws):
        for cp in w_copies(e, c, ws):
            cp.start()

    def wait_w(ws):
        for cp in w_copies(0, 0, ws):
            cp.wait()

    def gather_row(tile, sl, r):
        code = code_ref[tile * tm + r]
        tok = jnp.where(code < 2 * n, code >> 1, 0)
        pltpu.make_async_copy(hn_hbm.at[tok], xbuf.at[sl, r], gsem.at[sl]).start()

    def scatter_row(tile, sl, r, live):
        code = code_ref[tile * tm + r]
        dst = jnp.where(live, code, 2 * n + 2 * tm + r)
        pltpu.make_async_copy(ybuf.at[sl, r], o_hbm.at[dst], ssem.at[sl]).start()

    def wait_gather(sl):
        pltpu.make_async_copy(hn_hbm.at[pl.ds(0, tm)], xbuf.at[sl], gsem.at[sl]).wait()

    def wait_scatter(sl):
        pltpu.make_async_copy(ybuf.at[sl], o_hbm.at[pl.ds(0, tm)], ssem.at[sl]).wait()

    @pl.when(i == 0)
    def _():
        ybuf[...] = jnp.zeros(ybuf.shape, F32)
        for q in range(SPILL_TILES):
            cp = pltpu.make_async_copy(ybuf.at[0], o_hbm.at[pl.ds(2 * n + q * tm, tm)], ssem.at[0])
            cp.start()
            cp.wait()
        for c in range(ahead):
            start_w(te_ref[0], c, ring_slot(0, c))

        def body(r, carry):
            gather_row(0, 0, r)
            return carry

        lax.fori_loop(0, tm, body, 0, unroll=8)

    def tile_body(m):
        xb_ref[0:m] = xbuf[xs, 0:m].reshape(m, d).astype(BF16)
        e = te_ref[i]
        nxt = jnp.minimum(i + 1, na - 1)
        e_nxt = te_ref[nxt]
        prev = jnp.maximum(i - 1, 0)
        live = i > 0
        acc_ref[0:m] = jnp.zeros((m, d), F32)
        rpc = tm // nc

        def chunks(it, carry):
            for u in range(roll):
                c = it * roll + u
                ws = ring_slot(i, c)
                wait_w(ws)
                ca = c + ahead
                wrap = ca >= nc
                start_w(jnp.where(wrap, e_nxt, e), jnp.where(wrap, ca - nc, ca), ring_slot(i, ca))
                x = xb_ref[0:m]
                a = (_silu(_dot(x, wg_ring[ws].astype(BF16))) * _dot(x, wu_ring[ws].astype(BF16))).astype(BF16)
                acc_ref[0:m] += _dot(a, wd_ring[ws].astype(BF16))
                for q in range(rpc):
                    gather_row(nxt, xo, c * rpc + q)
                    scatter_row(prev, yp, c * rpc + q, live)
            return carry

        lax.fori_loop(0, nc // roll, chunks, 0)
        for r in range(nc * rpc, tm):
            gather_row(nxt, xo, r)
            scatter_row(prev, yp, r, live)

        @pl.when(i >= nb - 1)
        def _():
            wait_scatter(ys)

        ybuf[ys, 0:m] = acc_ref[0:m].reshape(m, d // LANES, LANES)

    @pl.when(i < na)
    def _():
        wait_gather(xs)
        real = nr_ref[i]
        lo = 0
        for m in m_steps:
            @pl.when(jnp.logical_and(real > lo, real <= m))
            def _(m=m):
                tile_body(m)

            lo = m

    @pl.when(i == na)
    def _():
        def body(r, carry):
            scatter_row(na - 1, yp, r, True)
            return carry

        lax.fori_loop(0, tm, body, 0, unroll=8)
        wait_scatter(yp)
        if nb == 2:
            wait_scatter(1 - yp)
        wait_gather(xs)
        for c in range(ahead):
            wait_w(ring_slot(na, c))


def _moe_ffn(hn, tile_expert, tile_rows, n_active, code, wg, wu, wd, *, layer, tm, cw, ring, roll, ybufs, m_steps):
    n, ncb, _ = hn.shape
    d = ncb * LANES
    rows = code.shape[0]
    hbm = pl.BlockSpec(memory_space=pl.ANY)
    return pl.pallas_call(
        functools.partial(_moe_ffn_kernel, layer=layer, n=n, tm=tm, cw=cw, roll=roll, m_steps=m_steps),
        grid_spec=pltpu.PrefetchScalarGridSpec(
            num_scalar_prefetch=4,
            grid=(rows // tm + 1,),
            in_specs=[hbm, hbm, hbm, hbm],
            out_specs=hbm,
            scratch_shapes=[
                pltpu.VMEM((2, tm, ncb, LANES), F32),
                pltpu.VMEM((ybufs, tm, ncb, LANES), F32),
                pltpu.VMEM((tm, d), BF16),
                pltpu.VMEM((tm, d), F32),
                pltpu.VMEM((ring, d, cw), F32),
                pltpu.VMEM((ring, d, cw), F32),
                pltpu.VMEM((ring, cw, d), F32),
                pltpu.SemaphoreType.DMA((2,)),
                pltpu.SemaphoreType.DMA((2,)),
                pltpu.SemaphoreType.DMA((ring,)),
            ],
        ),
        out_shape=jax.ShapeDtypeStruct((2 * n + SPILL_TILES * tm, ncb, LANES), F32),
        compiler_params=_cparams(("arbitrary",)),
        name="moe_ffn",
    )(tile_expert, tile_rows, n_active, code, hn, wg, wu, wd)


def _moe_final_kernel(h_ref, g_ref, gf_ref, y_ref, o_ref):
    o_ref[...] = _rms(_moe_combine(h_ref[...], g_ref, y_ref), gf_ref[...])


def _moe_final(h, gates_t, y, gf, *, td):
    n, d = h.shape
    ncb = d // LANES
    return pl.pallas_call(
        _moe_final_kernel,
        grid=(n // td,),
        in_specs=[
            pl.BlockSpec((td, d), lambda i: (i, 0)),
            pl.BlockSpec((td, 2), lambda i: (i, 0)),
            pl.BlockSpec((1, d), lambda i: (0, 0)),
            pl.BlockSpec((td, 2, ncb, LANES), lambda i: (i, 0, 0, 0)),
        ],
        out_specs=pl.BlockSpec((td, d), lambda i: (i, 0)),
        out_shape=jax.ShapeDtypeStruct((n, d), F32),
        compiler_params=_cparams(("arbitrary",)),
        name="moe_add",
    )(h, gates_t, gf.reshape(1, d), y.reshape(y.shape[0] // 2, 2, ncb, LANES))


MIX_TILE = 1024
MIX_SUB = 256
FFN_TM = 1024
FFN_SUB = 256
MOE_TM = 1024
MOE_CW = 256
MOE_RING = 4
MOE_ROLL = 1
MOE_YBUFS = 2
MOE_M_STEPS = (128, 256, 384, 512, 640, 768, 896, 1024)
ADD_TD = 1024


def _moe_rows(n, tm):
    return ((2 * n + N_EXPERTS * (tm - 1)) // tm) * tm


def kernel(x, norm_mix_e, w_in_e, sgu_norm, w_spatial, b_spatial, w_pool, pool_scale, w_out_e, norm_ffn_e, w_gate_d, w_up_d, w_down_d, norm_mix_o, w_pw1, b_pw1, w_dw, b_dw, ln_g, ln_b, w_pw2, b_pw2, norm_ffn_o, w_router, w_gate_x, w_up_x, w_down_x, norm_final):
    b, s, d = x.shape
    n = b * s
    depth = norm_mix_e.shape[0] + norm_mix_o.shape[0]
    if depth % 2 == 1:
        raise NotImplementedError("trunk depth must end on an odd (MoE) layer")
    rows = _moe_rows(n, MOE_TM)
    h = x
    moe = None
    for layer in range(depth):
        i = layer // 2
        if layer % 2 == 0:
            h = _even_mixer(h, norm_mix_e[i], w_in_e[i], sgu_norm[i], w_spatial[i], b_spatial[i], w_pool[i],
                            pool_scale[i], w_out_e[i], tile=MIX_TILE, sub=MIX_SUB, moe=moe)
            moe = None
            h = _dense_ffn(h.reshape(n, d), norm_ffn_e[i], w_gate_d[i], w_up_d[i], w_down_d[i],
                           tm=FFN_TM, sub=FFN_SUB).reshape(b, s, d)
        else:
            h, hn, ri, rg = _odd_mixer(h, norm_mix_o[i], w_pw1[i], b_pw1[i], w_dw[i], b_dw[i], ln_g[i], ln_b[i],
                                       w_pw2[i], b_pw2[i], norm_ffn_o[i], w_router[i], tile=MIX_TILE)
            pos, tab = _route_meta(ri, tm=MOE_TM)
            code = _route_inv(pos, tab[2, :N_EXPERTS], tab[3, :N_EXPERTS], n=n, rows=rows, tm=MOE_TM)
            nt = rows // MOE_TM
            ys = _moe_ffn(hn, tab[0, :nt], tab[4, :nt], tab[1, :1], code, w_gate_x, w_up_x, w_down_x,
                          layer=i, tm=MOE_TM, cw=MOE_CW, ring=MOE_RING, roll=MOE_ROLL, ybufs=MOE_YBUFS,
                          m_steps=MOE_M_STEPS)
            moe = (rg.T, ys)
    return _moe_final(h.reshape(n, d), *moe, norm_final, td=ADD_TD).reshape(b, s, d)
```
